```python
import jax, jax.numpy as jnp
from jax import lax
import numpy as np

D_MODEL = 1024
BATCH = 8
SEQ = 4096
DEPTH = 1

MEM_LEN = 256
GLA_HEADS = 4
GLA_DK = D_MODEL // 16
GLA_DV = D_MODEL // 8
GLA_KW = GLA_HEADS * GLA_DK
GLA_WIDTH = GLA_HEADS * GLA_DV
GLA_GATE_RANK = 16
GLA_TAU = 16.0
GLA_CHUNK = 64
CONV_CH = D_MODEL // 4
CONV_WIDTH = 3
XA_HEADS = 4
XA_DH = D_MODEL // 16
XA_WIDTH = XA_HEADS * XA_DH
MIX_WIDTH = GLA_WIDTH + CONV_CH + XA_WIDTH
IN_SIZES = (GLA_KW, GLA_KW, GLA_WIDTH, GLA_WIDTH, GLA_GATE_RANK,
            CONV_CH, CONV_CH, CONV_CH, XA_WIDTH)
IN_COLS = sum(IN_SIZES)
D_FF = 4 * D_MODEL
LN_EPS = 1e-5
RMS_EPS = 1e-6

kernel_name = "hybrid_gla_shortconv_memxattn_deepnorm"


def layer_norm(x, g, b):
    xf = x.astype(jnp.float32)
    mu = jnp.mean(xf, axis=-1, keepdims=True)
    var = jnp.mean(jnp.square(xf - mu), axis=-1, keepdims=True)
    y = (xf - mu) * lax.rsqrt(var + LN_EPS)
    return (y * g.astype(jnp.float32) + b.astype(jnp.float32)).astype(x.dtype)


def rms_norm(x, g):
    xf = x.astype(jnp.float32)
    y = xf * lax.rsqrt(jnp.mean(jnp.square(xf), axis=-1, keepdims=True) + RMS_EPS)
    return y * g.astype(jnp.float32)


def gla_chunked(q, k, v, log_a):
    B, S, H, dk = q.shape
    dv = v.shape[-1]
    C = GLA_CHUNK
    N = S // C

    def to_chunks(t):
        return t.astype(jnp.float32).reshape(B, N, C, H, t.shape[-1]).transpose(1, 0, 3, 2, 4)

    qc, kc, vc, ac = to_chunks(q), to_chunks(k), to_chunks(v), to_chunks(log_a)
    causal = jnp.tril(jnp.ones((C, C), dtype=bool))

    def step(state, inp):
        q_, k_, v_, a_ = inp
        b = jnp.cumsum(a_, axis=2)
        o_inter = jnp.einsum('bhck,bhkv->bhcv', q_ * jnp.exp(b), state)
        diff = b[:, :, :, None, :] - b[:, :, None, :, :]
        decay = jnp.exp(jnp.where(causal[:, :, None], diff, -jnp.inf))
        scores = jnp.einsum('bhik,bhjk,bhijk->bhij', q_, k_, decay)
        o_intra = jnp.einsum('bhij,bhjv->bhiv', scores, v_)
        b_last = b[:, :, -1:, :]
        k_dec = k_ * jnp.exp(b_last - b)
        state = (jnp.exp(b_last[:, :, 0, :])[..., None] * state
                 + jnp.einsum('bhck,bhcv->bhkv', k_dec, v_))
        return state, o_inter + o_intra

    s0 = jnp.zeros((B, H, dk, dv), jnp.float32)
    _, o = lax.scan(step, s0, (qc, kc, vc, ac))
    return o.transpose(1, 0, 3, 2, 4).reshape(B, S, H, dv)


def causal_short_conv(u, w):
    width = w.shape[0]
    S = u.shape[1]
    up = jnp.pad(u, ((0, 0), (width - 1, 0), (0, 0)))
    return sum(up[:, i:i + S, :] * w[i] for i in range(width))


def memory_cross_attention(q, mk, mv):
    s = jnp.einsum('bshd,bmhd->bhsm', q, mk).astype(jnp.float32) * (q.shape[-1] ** -0.5)
    p = jax.nn.softmax(s, axis=-1).astype(mv.dtype)
    return jnp.einsum('bhsm,bmhd->bshd', p, mv)


def hybrid_mixer(h, mem, w_in, w_gate_up, b_gate, conv_w, gla_norm_g, w_mem_kv, w_out):
    B, S, _ = h.shape
    z = h @ w_in
    offsets = [int(o) for o in np.cumsum(IN_SIZES)[:-1]]
    q, k, v, g, a_low, c_b, c_c, c_h, xq = jnp.split(z, offsets, axis=-1)

    q = q.reshape(B, S, GLA_HEADS, GLA_DK) * (GLA_DK ** -0.5)
    k = k.reshape(B, S, GLA_HEADS, GLA_DK)
    v = v.reshape(B, S, GLA_HEADS, GLA_DV)
    log_a = jax.nn.log_sigmoid((a_low @ w_gate_up + b_gate).astype(jnp.float32)) / GLA_TAU
    log_a = log_a.reshape(B, S, GLA_HEADS, GLA_DK)
    o = gla_chunked(q, k, v, log_a)
    o = rms_norm(o, gla_norm_g).astype(h.dtype)
    y_gla = (o * jax.nn.silu(g.reshape(B, S, GLA_HEADS, GLA_DV))).reshape(B, S, GLA_WIDTH)

    y_conv = c_b * causal_short_conv(c_c * c_h, conv_w)

    M = mem.shape[1]
    mk, mv = jnp.split(mem @ w_mem_kv, 2, axis=-1)
    y_xa = memory_cross_attention(xq.reshape(B, S, XA_HEADS, XA_DH),
                                  mk.reshape(B, M, XA_HEADS, XA_DH),
                                  mv.reshape(B, M, XA_HEADS, XA_DH)).reshape(B, S, XA_WIDTH)

    return jnp.concatenate([y_gla, y_conv, y_xa], axis=-1) @ w_out


def setup_inputs(seed: int = 0) -> dict:
    key = jax.random.key(seed)
    ks = jax.random.split(key, 20)
    f32 = jnp.float32
    beta = (8.0 * DEPTH) ** -0.25

    def nrm(k, shape, scale):
        return jax.random.normal(k, shape, f32) * scale

    L = DEPTH
    return {
        "x": jax.random.normal(ks[0], (BATCH, SEQ, D_MODEL), f32),
        "mem": jax.random.normal(ks[1], (BATCH, MEM_LEN, D_MODEL), f32),
        "w_in": nrm(ks[2], (L, D_MODEL, IN_COLS), D_MODEL ** -0.5),
        "w_gate_up": nrm(ks[3], (L, GLA_GATE_RANK, GLA_KW), GLA_GATE_RANK ** -0.5),
        "b_gate": nrm(ks[4], (L, GLA_KW), 0.1),
        "conv_w": nrm(ks[5], (L, CONV_WIDTH, CONV_CH), CONV_WIDTH ** -0.5),
        "gla_norm_g": 1.0 + nrm(ks[6], (L, GLA_DV), 0.02),
        "w_mem_kv": nrm(ks[7], (L, D_MODEL, 2 * XA_WIDTH), D_MODEL ** -0.5),
        "w_out": nrm(ks[8], (L, MIX_WIDTH, D_MODEL), beta * MIX_WIDTH ** -0.5),
        "ln1_g": 1.0 + nrm(ks[9], (L, D_MODEL), 0.02),
        "ln1_b": nrm(ks[10], (L, D_MODEL), 0.02),
        "w_ff1": nrm(ks[11], (L, D_MODEL, D_FF), D_MODEL ** -0.5),
        "w_ff2": nrm(ks[12], (L, D_FF, D_MODEL), beta * D_FF ** -0.5),
        "ln2_g": 1.0 + nrm(ks[13], (L, D_MODEL), 0.02),
        "ln2_b": nrm(ks[14], (L, D_MODEL), 0.02),
    }


def reference(x, mem, w_in, w_gate_up, b_gate, conv_w, gla_norm_g, w_mem_kv, w_out,
              ln1_g, ln1_b, w_ff1, w_ff2, ln2_g, ln2_b):
    alpha = (2.0 * DEPTH) ** 0.25
    for l in range(DEPTH):
        mix = hybrid_mixer(x, mem, w_in[l], w_gate_up[l], b_gate[l], conv_w[l],
                           gla_norm_g[l], w_mem_kv[l], w_out[l])
        x = layer_norm(alpha * x + mix, ln1_g[l], ln1_b[l])
        ff = jnp.square(jax.nn.relu(x @ w_ff1[l])) @ w_ff2[l]
        x = layer_norm(alpha * x + ff, ln2_g[l], ln2_b[l])
    return x
```

```python
import functools

import jax
import jax.numpy as jnp
from jax import lax
from jax.experimental import pallas as pl
from jax.experimental.pallas import tpu as pltpu

F32 = jnp.float32
BF16 = jnp.bfloat16

GLA_HEADS = 4
GLA_DK = 64
GLA_DV = 128
GLA_KW = GLA_HEADS * GLA_DK
GLA_WIDTH = GLA_HEADS * GLA_DV
GLA_GATE_RANK = 16
GLA_TAU = 16.0
CONV_CH = 256
CONV_TAPS = 3
XA_HEADS = 4
XA_DH = 64
XA_WIDTH = XA_HEADS * XA_DH
LN_EPS = 1e-5
RMS_EPS = 1e-6

LANES = 128
SUBLANES = 8
VMEM_LIMIT_BYTES = 56 * 1024 * 1024

OFF_Q = 0
OFF_K = OFF_Q + GLA_KW
OFF_V = OFF_K + GLA_KW
OFF_G = OFF_V + GLA_WIDTH
OFF_CB = OFF_G + GLA_WIDTH
OFF_CC = OFF_CB + CONV_CH
OFF_CH = OFF_CC + CONV_CH
OFF_XQ = OFF_CH + CONV_CH
OFF_A = OFF_XQ + XA_WIDTH
A_PAD = LANES
IN_COLS_PAD = OFF_A + A_PAD

MIX_TM = 512
GLA_C = 128
FFN_TM = 512
SAFE_DECAY = 60.0

_NT = (((1,), (1,)), ((), ()))


def _layer_norm(r, g, b):
    mu = jnp.mean(r, axis=-1, keepdims=True)
    d = r - mu
    var = jnp.mean(d * d, axis=-1, keepdims=True)
    return d * lax.rsqrt(var + LN_EPS) * g + b


def _mem_kv_kernel(mem_ref, w_ref, o_ref):
    o_ref[0] = jnp.dot(mem_ref[0].astype(BF16), w_ref[...],
                       preferred_element_type=F32).astype(BF16)


def _mixer_kernel(alpha, x_ref, win_ref, wg_ref, bg_ref, cw_ref, gn_ref, mkv_ref, wout_ref,
                  g1_ref, b1_ref, o_ref, z_ref, la_ref, b_ref, y_ref, u_ref, s_ref, sc_ref):
    tm = x_ref.shape[1]
    c = GLA_C

    @pl.when(pl.program_id(1) == 0)
    def _start_of_sequence():
        s_ref[...] = jnp.zeros_like(s_ref)
        u_ref[0:SUBLANES, :] = jnp.zeros((SUBLANES, CONV_CH), F32)

    x = x_ref[0]
    z_ref[...] = jnp.dot(x.astype(BF16), win_ref[...], preferred_element_type=F32)

    pre = jnp.dot(z_ref[:, OFF_A:OFF_A + A_PAD].astype(BF16), wg_ref[...],
                  preferred_element_type=F32) + bg_ref[...]
    log_sig = jnp.minimum(pre, 0.0) - jnp.log1p(jnp.exp(-jnp.abs(pre)))
    la_ref[...] = log_sig * (1.0 / GLA_TAU)

    lane = lax.broadcasted_iota(jnp.int32, (1, GLA_KW), 1)
    head_mask = [(lane // GLA_DK) == h for h in range(GLA_HEADS)]
    row_i = lax.broadcasted_iota(jnp.int32, (c, c), 0)
    col_i = lax.broadcasted_iota(jnp.int32, (c, c), 1)
    causal = row_i >= col_i
    tri = causal.astype(BF16)
    sub8 = lax.broadcasted_iota(jnp.int32, (SUBLANES, GLA_KW), 0)
    lane8 = lax.broadcasted_iota(jnp.int32, (SUBLANES, GLA_KW), 1)
    head_rows8 = ((lane8 // GLA_DK) == sub8).astype(F32)
    q_scale = GLA_DK ** -0.5

    def chunk_body(ci, carry):
        r0 = pl.multiple_of(ci * c, c)
        rows = pl.ds(r0, c)
        la = la_ref[rows, :]
        hi = la.astype(BF16)
        lo = (la - hi.astype(F32)).astype(BF16)
        b = (jnp.dot(tri, hi, preferred_element_type=F32)
             + jnp.dot(tri, lo, preferred_element_type=F32))
        b_ref[...] = b
        b_last = b[c - 1:c, :]
        kmat = z_ref[rows, OFF_K:OFF_K + GLA_KW]
        qt = z_ref[rows, OFF_Q:OFF_Q + GLA_KW] * (jnp.exp(b) * q_scale)
        kd = kmat * jnp.exp(b_last - b)
        qs = jnp.concatenate([jnp.where(head_mask[h], qt, 0.0) for h in range(GLA_HEADS)],
                             axis=0).astype(BF16)
        safe = jnp.max(-b_last) <= SAFE_DECAY

        @pl.when(safe)
        def _factorised_scores():
            kt = (kmat * jnp.exp(-b)).astype(BF16)
            sc_ref[...] = lax.dot_general(qs, kt, _NT, preferred_element_type=F32)

        @pl.when(jnp.logical_not(safe))
        def _rowwise_scores():
            def row_body(i, carry2):
                b_i = b_ref[pl.ds(i, 1), :]
                q_i = z_ref[pl.ds(r0 + i, 1), OFF_Q:OFF_Q + GLA_KW] * q_scale
                w = (kmat * jnp.exp(jnp.minimum(b_i - b, 0.0))).astype(BF16)
                q8 = (jnp.broadcast_to(q_i, (SUBLANES, GLA_KW)) * head_rows8).astype(BF16)
                r = lax.dot_general(q8, w, _NT, preferred_element_type=F32)
                for h in range(GLA_HEADS):
                    sc_ref[pl.ds(h * c + i, 1), :] = r[h:h + 1, :]
                return carry2
            lax.fori_loop(0, c, row_body, 0)

        s_old = s_ref[...]
        o_inter = jnp.dot(qs, s_old.astype(BF16), preferred_element_type=F32)
        kd_t = kd.T.astype(BF16)
        dec_t = jnp.broadcast_to(jnp.exp(b_last), (GLA_DV, GLA_KW)).T
        for h in range(GLA_HEADS):
            hr = slice(h * c, (h + 1) * c)
            kr = slice(h * GLA_DK, (h + 1) * GLA_DK)
            vc = slice(OFF_V + h * GLA_DV, OFF_V + (h + 1) * GLA_DV)
            gc = slice(OFF_G + h * GLA_DV, OFF_G + (h + 1) * GLA_DV)
            v_h = z_ref[rows, vc].astype(BF16)
            a_h = jnp.where(causal, sc_ref[hr, :], 0.0).astype(BF16)
            o_h = jnp.dot(a_h, v_h, preferred_element_type=F32) + o_inter[hr, :]
            ms = jnp.mean(o_h * o_h, axis=-1, keepdims=True)
            o_n = o_h * lax.rsqrt(ms + RMS_EPS) * gn_ref[...]
            g_h = z_ref[rows, gc]
            y_ref[rows, h * GLA_DV:(h + 1) * GLA_DV] = (
                o_n * (g_h * jax.nn.sigmoid(g_h))).astype(BF16)
            s_ref[kr, :] = s_old[kr, :] * dec_t[kr, :] + jnp.dot(
                kd_t[kr, :], v_h, preferred_element_type=F32)
        return carry

    lax.fori_loop(0, tm // c, chunk_body, 0)

    u = z_ref[:, OFF_CC:OFF_CC + CONV_CH] * z_ref[:, OFF_CH:OFF_CH + CONV_CH]
    u_ref[SUBLANES:SUBLANES + tm, :] = u
    u_m1 = u_ref[SUBLANES - 1:SUBLANES - 1 + tm, :]
    u_m2 = u_ref[SUBLANES - 2:SUBLANES - 2 + tm, :]
    conv = cw_ref[0:1, :] * u_m2 + cw_ref[1:2, :] * u_m1 + cw_ref[2:3, :] * u
    y_ref[:, GLA_WIDTH:GLA_WIDTH + CONV_CH] = (
        z_ref[:, OFF_CB:OFF_CB + CONV_CH] * conv).astype(BF16)
    u_ref[0:SUBLANES, :] = u_ref[tm:tm + SUBLANES, :]

    xq = z_ref[:, OFF_XQ:OFF_XQ + XA_WIDTH] * (XA_DH ** -0.5)
    mk = mkv_ref[0, :, 0:XA_WIDTH]
    mv = mkv_ref[0, :, XA_WIDTH:2 * XA_WIDTH].astype(F32)
    xa = jnp.zeros((tm, XA_WIDTH), F32)
    for h in range(XA_HEADS):
        q_h = jnp.where(head_mask[h], xq, 0.0).astype(BF16)
        s = lax.dot_general(q_h, mk, _NT, preferred_element_type=F32)
        e = jnp.exp(s - jnp.max(s, axis=-1, keepdims=True))
        p = (e / jnp.sum(e, axis=-1, keepdims=True)).astype(BF16)
        mv_h = jnp.where(head_mask[h], mv, 0.0).astype(BF16)
        xa = xa + jnp.dot(p, mv_h, preferred_element_type=F32)
    y_ref[:, GLA_WIDTH + CONV_CH:] = xa.astype(BF16)

    mix = jnp.dot(y_ref[...], wout_ref[...], preferred_element_type=F32)
    o_ref[0] = _layer_norm(alpha * x + mix, g1_ref[...], b1_ref[...])


def _ffn_kernel(alpha, x_ref, w1_ref, w2_ref, g2_ref, b2_ref, o_ref):
    x = x_ref[...]
    h = jnp.dot(x.astype(BF16), w1_ref[...], preferred_element_type=F32)
    h = jnp.square(jnp.maximum(h, 0.0)).astype(BF16)
    ff = jnp.dot(h, w2_ref[...], preferred_element_type=F32)
    o_ref[...] = _layer_norm(alpha * x + ff, g2_ref[...], b2_ref[...])


def _const_spec(shape):
    return pl.BlockSpec(shape, lambda *_: (0,) * len(shape))


def _mem_kv(mem, w_kv):
    bsz, m, d = mem.shape
    n = w_kv.shape[1]
    return pl.pallas_call(
        _mem_kv_kernel,
        grid=(bsz,),
        in_specs=[pl.BlockSpec((1, m, d), lambda b: (b, 0, 0)), _const_spec((d, n))],
        out_specs=pl.BlockSpec((1, m, n), lambda b: (b, 0, 0)),
        out_shape=jax.ShapeDtypeStruct((bsz, m, n), BF16),
        compiler_params=pltpu.CompilerParams(dimension_semantics=("arbitrary",)),
        name="mem_kv",
    )(mem, w_kv)


def _mixer(alpha, x, w_in_r, wg, bg, cw, gn, mkv, w_out, g1, b1):
    bsz, s, d = x.shape
    m = mkv.shape[1]
    tm = MIX_TM
    assert s % tm == 0 and tm % GLA_C == 0
    return pl.pallas_call(
        functools.partial(_mixer_kernel, alpha),
        grid=(bsz, s // tm),
        in_specs=[
            pl.BlockSpec((1, tm, d), lambda b, j: (b, j, 0)),
            _const_spec(w_in_r.shape), _const_spec(wg.shape), _const_spec(bg.shape),
            _const_spec(cw.shape), _const_spec(gn.shape),
            pl.BlockSpec((1, m, 2 * XA_WIDTH), lambda b, j: (b, 0, 0)),
            _const_spec(w_out.shape), _const_spec(g1.shape), _const_spec(b1.shape),
        ],
        out_specs=pl.BlockSpec((1, tm, d), lambda b, j: (b, j, 0)),
        out_shape=jax.ShapeDtypeStruct((bsz, s, d), F32),
        scratch_shapes=[
            pltpu.VMEM((tm, IN_COLS_PAD), F32),
            pltpu.VMEM((tm, GLA_KW), F32),
            pltpu.VMEM((GLA_C, GLA_KW), F32),
            pltpu.VMEM((tm, d), BF16),
            pltpu.VMEM((tm + SUBLANES, CONV_CH), F32),
            pltpu.VMEM((GLA_KW, GLA_DV), F32),
            pltpu.VMEM((GLA_HEADS * GLA_C, GLA_C), F32),
        ],
        compiler_params=pltpu.CompilerParams(
            dimension_semantics=("arbitrary", "arbitrary"),
            vmem_limit_bytes=VMEM_LIMIT_BYTES),
        name="mixer",
    )(x, w_in_r, wg, bg, cw, gn, mkv, w_out, g1, b1)


def _ffn(alpha, x2d, w1, w2, g2, b2):
    t, d = x2d.shape
    tm = FFN_TM
    assert t % tm == 0
    return pl.pallas_call(
        functools.partial(_ffn_kernel, alpha),
        grid=(t // tm,),
        in_specs=[
            pl.BlockSpec((tm, d), lambda i: (i, 0)),
            _const_spec(w1.shape), _const_spec(w2.shape),
            _const_spec(g2.shape), _const_spec(b2.shape),
        ],
        out_specs=pl.BlockSpec((tm, d), lambda i: (i, 0)),
        out_shape=jax.ShapeDtypeStruct((t, d), F32),
        compiler_params=pltpu.CompilerParams(
            dimension_semantics=("arbitrary",), vmem_limit_bytes=VMEM_LIMIT_BYTES),
        name="ffn",
    )(x2d, w1, w2, g2, b2)


def _reorder_w_in(w):
    sizes = (GLA_KW, GLA_KW, GLA_WIDTH, GLA_WIDTH, GLA_GATE_RANK,
             CONV_CH, CONV_CH, CONV_CH, XA_WIDTH)
    offs = [0]
    for sz in sizes:
        offs.append(offs[-1] + sz)
    q, k, v, g, a, c_b, c_c, c_h, xq = (w[:, offs[i]:offs[i + 1]] for i in range(len(sizes)))
    pad = jnp.zeros((w.shape[0], A_PAD - GLA_GATE_RANK), w.dtype)
    return jnp.concatenate([q, k, v, g, c_b, c_c, c_h, xq, a, pad], axis=1).astype(BF16)


def kernel(x, mem, w_in, w_gate_up, b_gate, conv_w, gla_norm_g, w_mem_kv, w_out,
           ln1_g, ln1_b, w_ff1, w_ff2, ln2_g, ln2_b):
    depth = w_in.shape[0]
    bsz, s, d = x.shape
    alpha = (2.0 * depth) ** 0.25
    for l in range(depth):
        w_in_r = _reorder_w_in(w_in[l])
        wg = jnp.zeros((A_PAD, GLA_KW), F32).at[:GLA_GATE_RANK].set(w_gate_up[l]).astype(BF16)
        mkv = _mem_kv(mem, w_mem_kv[l].astype(BF16))
        x = _mixer(alpha, x, w_in_r, wg, b_gate[l][None, :], conv_w[l], gla_norm_g[l][None, :],
                   mkv, w_out[l].astype(BF16), ln1_g[l][None, :], ln1_b[l][None, :])
        x = _ffn(alpha, x.reshape(bsz * s, d), w_ff1[l].astype(BF16), w_ff2[l].astype(BF16),
                 ln2_g[l][None, :], ln2_b[l][None, :]).reshape(bsz, s, d)
    return x
```

```python
import functools

import jax
import jax.numpy as jnp
from jax import lax
from jax.experimental import pallas as pl
from jax.experimental.pallas import tpu as pltpu

F32 = jnp.float32
BF16 = jnp.bfloat16

GLA_HEADS = 4
GLA_DK = 64
GLA_DV = 128
GLA_KW = GLA_HEADS * GLA_DK
GLA_WIDTH = GLA_HEADS * GLA_DV
GLA_GATE_RANK = 16
GLA_TAU = 16.0
CONV_CH = 256
CONV_TAPS = 3
XA_HEADS = 4
XA_DH = 64
XA_WIDTH = XA_HEADS * XA_DH
LN_EPS = 1e-5
RMS_EPS = 1e-6
IN_SIZES = (GLA_KW, GLA_KW, GLA_WIDTH, GLA_WIDTH, GLA_GATE_RANK,
            CONV_CH, CONV_CH, CONV_CH, XA_WIDTH)

SUBLANES = 8
VMEM_LIMIT_BYTES = 56 * 1024 * 1024

OFF_Q = 0
OFF_K = OFF_Q + GLA_KW
OFF_V = OFF_K + GLA_KW
OFF_G = OFF_V + GLA_WIDTH
OFF_CB = OFF_G + GLA_WIDTH
OFF_CC = OFF_CB + CONV_CH
OFF_CH = OFF_CC + CONV_CH
OFF_XQ = OFF_CH + CONV_CH
OFF_GATE = OFF_XQ + XA_WIDTH
IN_COLS_R = OFF_GATE + GLA_KW
GLA_COLS = OFF_CB

MIX_TM = 512
GLA_C = 128
OUT_SUB = 2
FFN_TM = 512
FFN_SUB = 2
SAFE_DECAY = 60.0

_NT = (((1,), (1,)), ((), ()))


def _layer_norm(r, g, b):
    mu = jnp.mean(r, axis=-1, keepdims=True)
    d = r - mu
    var = jnp.mean(d * d, axis=-1, keepdims=True)
    return d * lax.rsqrt(var + LN_EPS) * g + b


def _gate_w_kernel(wa_ref, wg_ref, o_ref):
    o_ref[...] = jnp.dot(wa_ref[...], wg_ref[...], preferred_element_type=F32,
                         precision=lax.Precision.HIGHEST).astype(BF16)


def _mem_kv_kernel(mem_ref, w_ref, o_ref):
    o_ref[0] = jnp.dot(mem_ref[0].astype(BF16), w_ref[...],
                       preferred_element_type=F32).astype(BF16)


def _head_masks():
    lane = lax.broadcasted_iota(jnp.int32, (1, GLA_KW), 1)
    return [(lane // GLA_DK) == h for h in range(GLA_HEADS)]


def _stack_heads(q, head_mask):
    return jnp.concatenate([jnp.where(m, q, 0.0) for m in head_mask], axis=0).astype(BF16)


def _gla_head_out(scores_h, o_inter_h, v_h, g_h, gn, causal):
    a_h = jnp.where(causal, scores_h, 0.0).astype(BF16)
    o_h = jnp.dot(a_h, v_h, preferred_element_type=F32) + o_inter_h
    ms = jnp.mean(o_h * o_h, axis=-1, keepdims=True)
    o_n = o_h * lax.rsqrt(ms + RMS_EPS) * gn
    return (o_n * (g_h * jax.nn.sigmoid(g_h))).astype(BF16)


def _mixer_kernel(alpha, x_ref, win_ref, bg_ref, cw_ref, gn_ref, mkv_ref, wout_ref,
                  g1_ref, b1_ref, o_ref, z_ref, b_ref, sst_ref, y_ref, u_ref, s_ref, sc_ref):
    tm = x_ref.shape[1]
    c = GLA_C
    n_chunks = tm // c
    chunk_rows = [slice(ci * c, (ci + 1) * c) for ci in range(n_chunks)]
    head_rows = [slice(h * c, (h + 1) * c) for h in range(GLA_HEADS)]
    head_keys = [slice(h * GLA_DK, (h + 1) * GLA_DK) for h in range(GLA_HEADS)]
    head_vals = [slice(h * GLA_DV, (h + 1) * GLA_DV) for h in range(GLA_HEADS)]

    @pl.when(pl.program_id(1) == 0)
    def _start_of_sequence():
        s_ref[...] = jnp.zeros_like(s_ref)
        u_ref[0:SUBLANES, :] = jnp.zeros((SUBLANES, CONV_CH), F32)

    xb = x_ref[0].astype(BF16)

    def proj(off, width):
        return jnp.dot(xb, win_ref[:, off:off + width], preferred_element_type=F32)

    head_mask = _head_masks()
    row_i = lax.broadcasted_iota(jnp.int32, (c, c), 0)
    col_i = lax.broadcasted_iota(jnp.int32, (c, c), 1)
    causal = row_i >= col_i
    tri = causal.astype(BF16)
    q_scale = GLA_DK ** -0.5
    gn = gn_ref[...]

    pre = proj(OFF_GATE, GLA_KW) + bg_ref[...]
    la = (jnp.minimum(pre, 0.0) - jnp.log1p(jnp.exp(-jnp.abs(pre)))) * (1.0 / GLA_TAU)
    zqk = proj(OFF_Q, 2 * GLA_KW)
    z_ref[:, OFF_Q:OFF_Q + 2 * GLA_KW] = zqk
    b_chunks = []
    for rows in chunk_rows:
        la_c = la[rows, :]
        hi = la_c.astype(BF16)
        lo = (la_c - hi.astype(F32)).astype(BF16)
        b = (jnp.dot(tri, hi, preferred_element_type=F32)
             + jnp.dot(tri, lo, preferred_element_type=F32))
        b_ref[rows, :] = b
        b_chunks.append(b)

    zvg = proj(OFF_V, 2 * GLA_WIDTH)
    z_ref[:, OFF_V:OFF_V + 2 * GLA_WIDTH] = zvg
    xq = proj(OFF_XQ, XA_WIDTH) * (XA_DH ** -0.5)

    worst = jnp.zeros((1, GLA_KW), F32)
    qs_chunks, scores_chunks, dec_chunks, upd_chunks = [], [], [], []
    for rows, b in zip(chunk_rows, b_chunks):
        b_last = b[c - 1:c, :]
        worst = jnp.maximum(worst, -b_last)
        kmat = zqk[rows, GLA_KW:]
        qs = _stack_heads(zqk[rows, :GLA_KW] * (jnp.exp(b) * q_scale), head_mask)
        kt = (kmat * jnp.exp(-b)).astype(BF16)
        kd_t = (kmat * jnp.exp(b_last - b)).T.astype(BF16)
        qs_chunks.append(qs)
        scores_chunks.append(lax.dot_general(qs, kt, _NT, preferred_element_type=F32))
        dec_chunks.append(jnp.broadcast_to(jnp.exp(b_last), (GLA_DV, GLA_KW)).T)
        upd_chunks.append(jnp.concatenate(
            [jnp.dot(kd_t[head_keys[h], :], zvg[rows, head_vals[h]].astype(BF16),
                     preferred_element_type=F32) for h in range(GLA_HEADS)], axis=0))

    mk = mkv_ref[0, :, 0:XA_WIDTH]
    mv = mkv_ref[0, :, XA_WIDTH:2 * XA_WIDTH].astype(F32)
    probs = []
    for h in range(XA_HEADS):
        q_h = jnp.where(head_mask[h], xq, 0.0).astype(BF16)
        s = lax.dot_general(q_h, mk, _NT, preferred_element_type=F32)
        e = jnp.exp(s - jnp.max(s, axis=-1, keepdims=True))
        probs.append((e * (1.0 / jnp.sum(e, axis=-1, keepdims=True))).astype(BF16))

    zc = proj(OFF_CB, 3 * CONV_CH)
    u = zc[:, CONV_CH:2 * CONV_CH] * zc[:, 2 * CONV_CH:]
    u_ref[SUBLANES:SUBLANES + tm, :] = u
    u_m1 = u_ref[SUBLANES - 1:SUBLANES - 1 + tm, :]
    u_m2 = u_ref[SUBLANES - 2:SUBLANES - 2 + tm, :]
    conv = cw_ref[0:1, :] * u_m2 + cw_ref[1:2, :] * u_m1 + cw_ref[2:3, :] * u
    y_ref[:, GLA_WIDTH:GLA_WIDTH + CONV_CH] = (zc[:, 0:CONV_CH] * conv).astype(BF16)
    u_ref[0:SUBLANES, :] = u_ref[tm:tm + SUBLANES, :]

    s_c = s_ref[...]
    for ci, rows in enumerate(chunk_rows):
        sst_ref[ci] = s_c
        o_inter = jnp.dot(qs_chunks[ci], s_c.astype(BF16), preferred_element_type=F32)
        for h in range(GLA_HEADS):
            y_ref[rows, head_vals[h]] = _gla_head_out(
                scores_chunks[ci][head_rows[h], :], o_inter[head_rows[h], :],
                zvg[rows, head_vals[h]].astype(BF16),
                zvg[rows, GLA_WIDTH + h * GLA_DV:GLA_WIDTH + (h + 1) * GLA_DV], gn, causal)
        s_c = s_c * dec_chunks[ci] + upd_chunks[ci]
    s_ref[...] = s_c

    xa = jnp.zeros((tm, XA_WIDTH), F32)
    for h in range(XA_HEADS):
        mv_h = jnp.where(head_mask[h], mv, 0.0).astype(BF16)
        xa = xa + jnp.dot(probs[h], mv_h, preferred_element_type=F32)
    y_ref[:, GLA_WIDTH + CONV_CH:] = xa.astype(BF16)

    def project_out():
        sub = tm // OUT_SUB
        for i in range(OUT_SUB):
            rows = slice(i * sub, (i + 1) * sub)
            mix = jnp.dot(y_ref[rows, :], wout_ref[...], preferred_element_type=F32)
            o_ref[0, rows, :] = _layer_norm(alpha * x_ref[0, rows, :] + mix,
                                            g1_ref[...], b1_ref[...])

    project_out()

    @pl.when(jnp.max(worst) > SAFE_DECAY)
    def _redo_with_rowwise_scores():
        sub8 = lax.broadcasted_iota(jnp.int32, (SUBLANES, GLA_KW), 0)
        lane8 = lax.broadcasted_iota(jnp.int32, (SUBLANES, GLA_KW), 1)
        head_sel8 = ((lane8 // GLA_DK) == sub8).astype(F32)

        def chunk_body(ci, carry):
            r0 = pl.multiple_of(ci * c, c)
            rows = pl.ds(r0, c)
            b = b_ref[rows, :]
            kmat = z_ref[rows, OFF_K:OFF_K + GLA_KW]

            def row_body(i, carry2):
                b_i = b_ref[pl.ds(r0 + i, 1), :]
                q_i = z_ref[pl.ds(r0 + i, 1), OFF_Q:OFF_Q + GLA_KW] * q_scale
                w = (kmat * jnp.exp(jnp.minimum(b_i - b, 0.0))).astype(BF16)
                q8 = (jnp.broadcast_to(q_i, (SUBLANES, GLA_KW)) * head_sel8).astype(BF16)
                r = lax.dot_general(q8, w, _NT, preferred_element_type=F32)
                for h in range(GLA_HEADS):
                    sc_ref[pl.ds(h * c + i, 1), :] = r[h:h + 1, :]
                return carry2

            lax.fori_loop(0, c, row_body, 0)
            qs = _stack_heads(z_ref[rows, OFF_Q:OFF_Q + GLA_KW] * (jnp.exp(b) * q_scale),
                              head_mask)
            o_inter = jnp.dot(qs, sst_ref[ci].astype(BF16), preferred_element_type=F32)
            for h in range(GLA_HEADS):
                v_h = z_ref[rows, OFF_V + h * GLA_DV:OFF_V + (h + 1) * GLA_DV].astype(BF16)
                g_h = z_ref[rows, OFF_G + h * GLA_DV:OFF_G + (h + 1) * GLA_DV]
                y_ref[rows, head_vals[h]] = _gla_head_out(
                    sc_ref[head_rows[h], :], o_inter[head_rows[h], :], v_h, g_h, gn, causal)
            return carry

        lax.fori_loop(0, n_chunks, chunk_body, 0)
        project_out()


def _ffn_kernel(alpha, x_ref, w1_ref, w2_ref, g2_ref, b2_ref, o_ref):
    sub = x_ref.shape[0] // FFN_SUB
    for i in range(FFN_SUB):
        rows = slice(i * sub, (i + 1) * sub)
        x = x_ref[rows, :]
        h = jnp.dot(x.astype(BF16), w1_ref[...], preferred_element_type=F32)
        h = jnp.square(jnp.maximum(h, 0.0)).astype(BF16)
        ff = jnp.dot(h, w2_ref[...], preferred_element_type=F32)
        o_ref[rows, :] = _layer_norm(alpha * x + ff, g2_ref[...], b2_ref[...])


def _const_spec(shape):
    return pl.BlockSpec(shape, lambda *_: (0,) * len(shape))


def _gate_w(w_a, w_g):
    return pl.pallas_call(
        _gate_w_kernel,
        out_shape=jax.ShapeDtypeStruct((w_a.shape[0], w_g.shape[1]), BF16),
        name="gate_w",
    )(w_a, w_g)


def _mem_kv(mem, w_kv):
    bsz, m, d = mem.shape
    n = w_kv.shape[1]
    return pl.pallas_call(
        _mem_kv_kernel,
        grid=(bsz,),
        in_specs=[pl.BlockSpec((1, m, d), lambda b: (b, 0, 0)), _const_spec((d, n))],
        out_specs=pl.BlockSpec((1, m, n), lambda b: (b, 0, 0)),
        out_shape=jax.ShapeDtypeStruct((bsz, m, n), BF16),
        compiler_params=pltpu.CompilerParams(dimension_semantics=("arbitrary",)),
        name="mem_kv",
    )(mem, w_kv)


def _mixer(alpha, x, w_in_r, bg, cw, gn, mkv, w_out, g1, b1):
    bsz, s, d = x.shape
    m = mkv.shape[1]
    tm = MIX_TM
    assert s % tm == 0 and tm % GLA_C == 0 and tm % OUT_SUB == 0
    return pl.pallas_call(
        functools.partial(_mixer_kernel, alpha),
        grid=(bsz, s // tm),
        in_specs=[
            pl.BlockSpec((1, tm, d), lambda b, j: (b, j, 0)),
            _const_spec(w_in_r.shape), _const_spec(bg.shape),
            _const_spec(cw.shape), _const_spec(gn.shape),
            pl.BlockSpec((1, m, 2 * XA_WIDTH), lambda b, j: (b, 0, 0)),
            _const_spec(w_out.shape), _const_spec(g1.shape), _const_spec(b1.shape),
        ],
        out_specs=pl.BlockSpec((1, tm, d), lambda b, j: (b, j, 0)),
        out_shape=jax.ShapeDtypeStruct((bsz, s, d), F32),
        scratch_shapes=[
            pltpu.VMEM((tm, GLA_COLS), F32),
            pltpu.VMEM((tm, GLA_KW), F32),
            pltpu.VMEM((tm // GLA_C, GLA_KW, GLA_DV), F32),
            pltpu.VMEM((tm, d), BF16),
            pltpu.VMEM((tm + SUBLANES, CONV_CH), F32),
            pltpu.VMEM((GLA_KW, GLA_DV), F32),
            pltpu.VMEM((GLA_HEADS * GLA_C, GLA_C), F32),
        ],
        compiler_params=pltpu.CompilerParams(
            dimension_semantics=("arbitrary", "arbitrary"),
            vmem_limit_bytes=VMEM_LIMIT_BYTES),
        name="mixer",
    )(x, w_in_r, bg, cw, gn, mkv, w_out, g1, b1)


def _ffn(alpha, x2d, w1, w2, g2, b2):
    t, d = x2d.shape
    tm = FFN_TM
    assert t % tm == 0 and tm % FFN_SUB == 0
    return pl.pallas_call(
        functools.partial(_ffn_kernel, alpha),
        grid=(t // tm,),
        in_specs=[
            pl.BlockSpec((tm, d), lambda i: (i, 0)),
            _const_spec(w1.shape), _const_spec(w2.shape),
            _const_spec(g2.shape), _const_spec(b2.shape),
        ],
        out_specs=pl.BlockSpec((tm, d), lambda i: (i, 0)),
        out_shape=jax.ShapeDtypeStruct((t, d), F32),
        compiler_params=pltpu.CompilerParams(
            dimension_semantics=("arbitrary",), vmem_limit_bytes=VMEM_LIMIT_BYTES),
        name="ffn",
    )(x2d, w1, w2, g2, b2)


def _split_w_in(w):
    offs = [0]
    for sz in IN_SIZES:
        offs.append(offs[-1] + sz)
    return [w[:, offs[i]:offs[i + 1]] for i in range(len(IN_SIZES))]


def kernel(x, mem, w_in, w_gate_up, b_gate, conv_w, gla_norm_g, w_mem_kv, w_out,
           ln1_g, ln1_b, w_ff1, w_ff2, ln2_g, ln2_b):
    depth = w_in.shape[0]
    bsz, s, d = x.shape
    alpha = (2.0 * depth) ** 0.25
    for l in range(depth):
        q, k, v, g, a, c_b, c_c, c_h, xq = _split_w_in(w_in[l])
        w_gate = _gate_w(a, w_gate_up[l])
        w_in_r = jnp.concatenate(
            [t.astype(BF16) for t in (q, k, v, g, c_b, c_c, c_h, xq)] + [w_gate], axis=1)
        mkv = _mem_kv(mem, w_mem_kv[l].astype(BF16))
        x = _mixer(alpha, x, w_in_r, b_gate[l][None, :], conv_w[l], gla_norm_g[l][None, :],
                   mkv, w_out[l].astype(BF16), ln1_g[l][None, :], ln1_b[l][None, :])
        x = _ffn(alpha, x.reshape(bsz * s, d), w_ff1[l].astype(BF16), w_ff2[l].astype(BF16),
                 ln2_g[l][None, :], ln2_b[l][None, :]).reshape(bsz, s, d)
    return x
```

```python
import functools

import jax
import jax.numpy as jnp
from jax import lax
from jax.experimental import pallas as pl
from jax.experimental.pallas import tpu as pltpu

F32 = jnp.float32
BF16 = jnp.bfloat16

GLA_HEADS = 4
GLA_DK = 64
GLA_DV = 128
GLA_KW = GLA_HEADS * GLA_DK
GLA_WIDTH = GLA_HEADS * GLA_DV
GLA_GATE_RANK = 16
GLA_TAU = 16.0
CONV_CH = 256
CONV_TAPS = 3
XA_HEADS = 4
XA_DH = 64
XA_WIDTH = XA_HEADS * XA_DH
LN_EPS = 1e-5
RMS_EPS = 1e-6
IN_SIZES = (GLA_KW, GLA_KW, GLA_WIDTH, GLA_WIDTH, GLA_GATE_RANK,
            CONV_CH, CONV_CH, CONV_CH, XA_WIDTH)

LANES = 128
SUBLANES = 8
VMEM_LIMIT_BYTES = 56 * 1024 * 1024

OFF_Q = 0
OFF_K = OFF_Q + GLA_KW
OFF_V = OFF_K + GLA_KW
OFF_G = OFF_V + GLA_WIDTH
OFF_CB = OFF_G + GLA_WIDTH
OFF_CC = OFF_CB + CONV_CH
OFF_CH = OFF_CC + CONV_CH
OFF_XQ = OFF_CH + CONV_CH
OFF_GATE = OFF_XQ + XA_WIDTH
IN_COLS_R = OFF_GATE + GLA_KW
GLA_COLS = OFF_CB

MIX_TM = 512
GLA_C = 128
OUT_SUB = 2
FFN_TM = 1024
FFN_SUB = 4
SAFE_DECAY = 60.0

_NT = (((1,), (1,)), ((), ()))


def _layer_norm(r, g, b):
    mu = jnp.mean(r, axis=-1, keepdims=True)
    d = r - mu
    var = jnp.mean(d * d, axis=-1, keepdims=True)
    return d * lax.rsqrt(var + LN_EPS) * g + b


def _gate_w_kernel(wa_ref, wg_ref, o_ref):
    o_ref[...] = jnp.dot(wa_ref[...], wg_ref[...], preferred_element_type=F32,
                         precision=lax.Precision.HIGHEST).astype(BF16)


def _head_masks():
    lane = lax.broadcasted_iota(jnp.int32, (1, GLA_KW), 1)
    return [(lane // GLA_DK) == h for h in range(GLA_HEADS)]


def _stack_heads(q, head_mask):
    return jnp.concatenate([jnp.where(m, q, 0.0) for m in head_mask], axis=0).astype(BF16)


def _gla_head_out(scores_h, o_inter_h, v_h, g_h, gn, causal):
    a_h = jnp.where(causal, scores_h, 0.0).astype(BF16)
    o_h = jnp.dot(a_h, v_h, preferred_element_type=F32) + o_inter_h
    ms = jnp.mean(o_h * o_h, axis=-1, keepdims=True)
    o_n = o_h * lax.rsqrt(ms + RMS_EPS) * gn
    return (o_n * (g_h * jax.nn.sigmoid(g_h))).astype(BF16)


def _mixer_kernel(alpha, x_ref, win_ref, bg_ref, cw_ref, gn_ref, mem_ref, wkv_ref, wout_ref,
                  g1_ref, b1_ref, o_ref, z_ref, b_ref, sst_ref, y_ref, u_ref, s_ref, sc_ref,
                  mkv_ref):
    tm = x_ref.shape[1]
    c = GLA_C
    n_chunks = tm // c
    chunk_rows = [slice(ci * c, (ci + 1) * c) for ci in range(n_chunks)]
    head_rows = [slice(h * c, (h + 1) * c) for h in range(GLA_HEADS)]
    head_keys = [slice(h * GLA_DK, (h + 1) * GLA_DK) for h in range(GLA_HEADS)]
    head_vals = [slice(h * GLA_DV, (h + 1) * GLA_DV) for h in range(GLA_HEADS)]

    @pl.when(pl.program_id(1) == 0)
    def _start_of_sequence():
        s_ref[...] = jnp.zeros_like(s_ref)
        u_ref[0:SUBLANES, :] = jnp.zeros((SUBLANES, CONV_CH), F32)
        mkv_ref[...] = jnp.dot(mem_ref[0].astype(BF16), wkv_ref[...],
                               preferred_element_type=F32).astype(BF16)

    xb = x_ref[0].astype(BF16)

    def proj(off, width):
        return jnp.dot(xb, win_ref[:, off:off + width], preferred_element_type=F32)

    head_mask = _head_masks()
    row_i = lax.broadcasted_iota(jnp.int32, (c, c), 0)
    col_i = lax.broadcasted_iota(jnp.int32, (c, c), 1)
    causal = row_i >= col_i
    tri = causal.astype(BF16)
    q_scale = GLA_DK ** -0.5
    gn = gn_ref[...]

    pre = proj(OFF_GATE, GLA_KW) + bg_ref[...]
    la = (jnp.minimum(pre, 0.0) - jnp.log1p(jnp.exp(-jnp.abs(pre)))) * (1.0 / GLA_TAU)
    zqk = proj(OFF_Q, 2 * GLA_KW)
    z_ref[:, OFF_Q:OFF_Q + 2 * GLA_KW] = zqk
    b_chunks = []
    for rows in chunk_rows:
        la_c = la[rows, :]
        hi = la_c.astype(BF16)
        lo = (la_c - hi.astype(F32)).astype(BF16)
        b = (jnp.dot(tri, hi, preferred_element_type=F32)
             + jnp.dot(tri, lo, preferred_element_type=F32))
        b_ref[rows, :] = b
        b_chunks.append(b)

    zvg = proj(OFF_V, 2 * GLA_WIDTH)
    z_ref[:, OFF_V:OFF_V + 2 * GLA_WIDTH] = zvg
    xq = proj(OFF_XQ, XA_WIDTH) * (XA_DH ** -0.5)

    worst = jnp.zeros((1, GLA_KW), F32)
    qs_chunks, scores_chunks, dec_chunks, upd_chunks = [], [], [], []
    for rows, b in zip(chunk_rows, b_chunks):
        b_last = b[c - 1:c, :]
        worst = jnp.maximum(worst, -b_last)
        kmat = zqk[rows, GLA_KW:]
        qs = _stack_heads(zqk[rows, :GLA_KW] * (jnp.exp(b) * q_scale), head_mask)
        kt = (kmat * jnp.exp(-b)).astype(BF16)
        kd_t = (kmat * jnp.exp(b_last - b)).T.astype(BF16)
        qs_chunks.append(qs)
        scores_chunks.append(lax.dot_general(qs, kt, _NT, preferred_element_type=F32))
        dec_chunks.append(jnp.broadcast_to(jnp.exp(b_last), (GLA_DV, GLA_KW)).T)
        upd_chunks.append(jnp.concatenate(
            [jnp.dot(kd_t[head_keys[h], :], zvg[rows, head_vals[h]].astype(BF16),
                     preferred_element_type=F32) for h in range(GLA_HEADS)], axis=0))

    mk = mkv_ref[:, 0:XA_WIDTH]
    mv = mkv_ref[:, XA_WIDTH:2 * XA_WIDTH].astype(F32)
    probs = []
    for h in range(XA_HEADS):
        q_h = jnp.where(head_mask[h], xq, 0.0).astype(BF16)
        s = lax.dot_general(q_h, mk, _NT, preferred_element_type=F32)
        e = jnp.exp(s - jnp.max(s, axis=-1, keepdims=True))
        probs.append((e * (1.0 / jnp.sum(e, axis=-1, keepdims=True))).astype(BF16))


    s_c = s_ref[...]
    for ci, rows in enumerate(chunk_rows):
        sst_ref[ci] = s_c
        o_inter = jnp.dot(qs_chunks[ci], s_c.astype(BF16), preferred_element_type=F32)
        for h in range(GLA_HEADS):
            y_ref[rows, head_vals[h]] = _gla_head_out(
                scores_chunks[ci][head_rows[h], :], o_inter[head_rows[h], :],
                zvg[rows, head_vals[h]].astype(BF16),
                zvg[rows, GLA_WIDTH + h * GLA_DV:GLA_WIDTH + (h + 1) * GLA_DV], gn, causal)
        s_c = s_c * dec_chunks[ci] + upd_chunks[ci]
    s_ref[...] = s_c

    zc = proj(OFF_CB, 3 * CONV_CH)
    u = zc[:, CONV_CH:2 * CONV_CH] * zc[:, 2 * CONV_CH:]
    u_ref[SUBLANES:SUBLANES + tm, :] = u
    u_m1 = u_ref[SUBLANES - 1:SUBLANES - 1 + tm, :]
    u_m2 = u_ref[SUBLANES - 2:SUBLANES - 2 + tm, :]
    conv = cw_ref[0:1, :] * u_m2 + cw_ref[1:2, :] * u_m1 + cw_ref[2:3, :] * u
    y_conv = (zc[:, 0:CONV_CH] * conv).astype(BF16)
    y_ref[:, GLA_WIDTH:GLA_WIDTH + CONV_CH] = y_conv
    u_ref[0:SUBLANES, :] = u_ref[tm:tm + SUBLANES, :]

    mix_gla = [jnp.dot(y_ref[rows, 0:GLA_WIDTH], wout_ref[0:GLA_WIDTH, :],
                       preferred_element_type=F32) for rows in chunk_rows]

    xa = jnp.zeros((tm, XA_WIDTH), F32)
    for h in range(XA_HEADS):
        mv_h = jnp.where(head_mask[h], mv, 0.0).astype(BF16)
        xa = xa + jnp.dot(probs[h], mv_h, preferred_element_type=F32)
    y_xa = xa.astype(BF16)
    y_ref[:, GLA_WIDTH + CONV_CH:] = y_xa

    sub = tm // OUT_SUB
    for i in range(OUT_SUB):
        rows = slice(i * sub, (i + 1) * sub)
        mix = (jnp.concatenate(mix_gla[i * sub // c:(i + 1) * sub // c], axis=0)
               + jnp.dot(y_conv[rows, :], wout_ref[GLA_WIDTH:GLA_WIDTH + CONV_CH, :],
                         preferred_element_type=F32)
               + jnp.dot(y_xa[rows, :], wout_ref[GLA_WIDTH + CONV_CH:, :],
                         preferred_element_type=F32))
        o_ref[0, rows, :] = _layer_norm(alpha * x_ref[0, rows, :] + mix,
                                        g1_ref[...], b1_ref[...])

    def project_out():
        mix = jnp.dot(y_ref[...], wout_ref[...], preferred_element_type=F32)
        o_ref[0] = _layer_norm(alpha * x_ref[0] + mix, g1_ref[...], b1_ref[...])

    @pl.when(jnp.max(worst) > SAFE_DECAY)
    def _redo_with_rowwise_scores():
        sub8 = lax.broadcasted_iota(jnp.int32, (SUBLANES, GLA_KW), 0)
        lane8 = lax.broadcasted_iota(jnp.int32, (SUBLANES, GLA_KW), 1)
        head_sel8 = ((lane8 // GLA_DK) == sub8).astype(F32)

        def chunk_body(ci, carry):
            r0 = pl.multiple_of(ci * c, c)
            rows = pl.ds(r0, c)
            b = b_ref[rows, :]
            kmat = z_ref[rows, OFF_K:OFF_K + GLA_KW]

            def row_body(i, carry2):
                b_i = b_ref[pl.ds(r0 + i, 1), :]
                q_i = z_ref[pl.ds(r0 + i, 1), OFF_Q:OFF_Q + GLA_KW] * q_scale
                w = (kmat * jnp.exp(jnp.minimum(b_i - b, 0.0))).astype(BF16)
                q8 = (jnp.broadcast_to(q_i, (SUBLANES, GLA_KW)) * head_sel8).astype(BF16)
                r = lax.dot_general(q8, w, _NT, preferred_element_type=F32)
                for h in range(GLA_HEADS):
                    sc_ref[pl.ds(h * c + i, 1), :] = r[h:h + 1, :]
                return carry2

            lax.fori_loop(0, c, row_body, 0)
            qs = _stack_heads(z_ref[rows, OFF_Q:OFF_Q + GLA_KW] * (jnp.exp(b) * q_scale),
                              head_mask)
            o_inter = jnp.dot(qs, sst_ref[ci].astype(BF16), preferred_element_type=F32)
            for h in range(GLA_HEADS):
                v_h = z_ref[rows, OFF_V + h * GLA_DV:OFF_V + (h + 1) * GLA_DV].astype(BF16)
                g_h = z_ref[rows, OFF_G + h * GLA_DV:OFF_G + (h + 1) * GLA_DV]
                y_ref[rows, head_vals[h]] = _gla_head_out(
                    sc_ref[head_rows[h], :], o_inter[head_rows[h], :], v_h, g_h, gn, causal)
            return carry

        lax.fori_loop(0, n_chunks, chunk_body, 0)
        project_out()


def _zero_after(src):
    r, n = src.shape
    acc = src[:, 0:LANES]
    for j in range(1, n // LANES):
        acc = acc + src[:, j * LANES:(j + 1) * LANES]
    t = acc[0:SUBLANES, :]
    for i in range(1, r // SUBLANES):
        t = t + acc[i * SUBLANES:(i + 1) * SUBLANES, :]
    u = lax.bitcast_convert_type(t, jnp.uint32)
    return lax.shift_right_logical(lax.shift_right_logical(u, jnp.uint32(16)), jnp.uint32(16))


def _tie(x, zero):
    blk = lax.bitcast_convert_type(
        lax.bitcast_convert_type(x[0:SUBLANES, 0:LANES], jnp.uint32) | zero, x.dtype)
    col = jnp.concatenate([blk, x[SUBLANES:, 0:LANES]], axis=0)
    return jnp.concatenate([col, x[:, LANES:]], axis=1)


def _ffn_kernel(alpha, x_ref, w1_ref, w2_ref, g2_ref, b2_ref, o_ref):
    sub = x_ref.shape[0] // FFN_SUB
    for i in range(FFN_SUB):
        rows = slice(i * sub, (i + 1) * sub)
        x = x_ref[rows, :]
        h = jnp.dot(x.astype(BF16), w1_ref[...], preferred_element_type=F32)
        h = jnp.square(jnp.maximum(h, 0.0)).astype(BF16)
        ff = jnp.dot(h, w2_ref[...], preferred_element_type=F32)
        o_ref[rows, :] = _layer_norm(alpha * x + ff, g2_ref[...], b2_ref[...])


def _const_spec(shape):
    return pl.BlockSpec(shape, lambda *_: (0,) * len(shape))


def _gate_w(w_a, w_g):
    return pl.pallas_call(
        _gate_w_kernel,
        out_shape=jax.ShapeDtypeStruct((w_a.shape[0], w_g.shape[1]), BF16),
        name="gate_w",
    )(w_a, w_g)


def _mixer(alpha, x, w_in_r, bg, cw, gn, mem, w_kv, w_out, g1, b1):
    bsz, s, d = x.shape
    m = mem.shape[1]
    tm = MIX_TM
    assert s % tm == 0 and tm % GLA_C == 0 and (tm // OUT_SUB) % GLA_C == 0
    return pl.pallas_call(
        functools.partial(_mixer_kernel, alpha),
        grid=(bsz, s // tm),
        in_specs=[
            pl.BlockSpec((1, tm, d), lambda b, j: (b, j, 0)),
            _const_spec(w_in_r.shape), _const_spec(bg.shape),
            _const_spec(cw.shape), _const_spec(gn.shape),
            pl.BlockSpec((1, m, d), lambda b, j: (b, 0, 0)),
            _const_spec(w_kv.shape), _const_spec(w_out.shape),
            _const_spec(g1.shape), _const_spec(b1.shape),
        ],
        out_specs=pl.BlockSpec((1, tm, d), lambda b, j: (b, j, 0)),
        out_shape=jax.ShapeDtypeStruct((bsz, s, d), F32),
        scratch_shapes=[
            pltpu.VMEM((tm, GLA_COLS), F32),
            pltpu.VMEM((tm, GLA_KW), F32),
            pltpu.VMEM((tm // GLA_C, GLA_KW, GLA_DV), F32),
            pltpu.VMEM((tm, d), BF16),
            pltpu.VMEM((tm + SUBLANES, CONV_CH), F32),
            pltpu.VMEM((GLA_KW, GLA_DV), F32),
            pltpu.VMEM((GLA_HEADS * GLA_C, GLA_C), F32),
            pltpu.VMEM((m, 2 * XA_WIDTH), BF16),
        ],
        compiler_params=pltpu.CompilerParams(
            dimension_semantics=("arbitrary", "arbitrary"),
            vmem_limit_bytes=VMEM_LIMIT_BYTES),
        name="mixer",
    )(x, w_in_r, bg, cw, gn, mem, w_kv, w_out, g1, b1)


def _ffn(alpha, x2d, w1, w2, g2, b2):
    t, d = x2d.shape
    tm = FFN_TM
    assert t % tm == 0 and tm % FFN_SUB == 0
    return pl.pallas_call(
        functools.partial(_ffn_kernel, alpha),
        grid=(t // tm,),
        in_specs=[
            pl.BlockSpec((tm, d), lambda i: (i, 0)),
            _const_spec(w1.shape), _const_spec(w2.shape),
            _const_spec(g2.shape), _const_spec(b2.shape),
        ],
        out_specs=pl.BlockSpec((tm, d), lambda i: (i, 0)),
        out_shape=jax.ShapeDtypeStruct((t, d), F32),
        compiler_params=pltpu.CompilerParams(
            dimension_semantics=("arbitrary",), vmem_limit_bytes=VMEM_LIMIT_BYTES),
        name="ffn",
    )(x2d, w1, w2, g2, b2)


def _split_w_in(w):
    offs = [0]
    for sz in IN_SIZES:
        offs.append(offs[-1] + sz)
    return [w[:, offs[i]:offs[i + 1]] for i in range(len(IN_SIZES))]


def kernel(x, mem, w_in, w_gate_up, b_gate, conv_w, gla_norm_g, w_mem_kv, w_out,
           ln1_g, ln1_b, w_ff1, w_ff2, ln2_g, ln2_b):
    depth = w_in.shape[0]
    bsz, s, d = x.shape
    alpha = (2.0 * depth) ** 0.25
    for l in range(depth):
        q, k, v, g, a, c_b, c_c, c_h, xq = _split_w_in(w_in[l])
        w_gate = _gate_w(a, w_gate_up[l])
        w_in_r = jnp.concatenate(
            [t.astype(BF16) for t in (q, k, v, g, c_b, c_c, c_h, xq)] + [w_gate], axis=1)
        x = _mixer(alpha, x, w_in_r, b_gate[l][None, :], conv_w[l], gla_norm_g[l][None, :],
                   mem, w_mem_kv[l].astype(BF16), w_out[l].astype(BF16),
                   ln1_g[l][None, :], ln1_b[l][None, :])
        x = _ffn(alpha, x.reshape(bsz * s, d), w_ff1[l].astype(BF16), w_ff2[l].astype(BF16),
                 ln2_g[l][None, :], ln2_b[l][None, :]).reshape(bsz, s, d)
    return x
```

```python
import functools

import jax
import jax.numpy as jnp
from jax import lax
from jax.experimental import pallas as pl
from jax.experimental.pallas import tpu as pltpu

F32 = jnp.float32
BF16 = jnp.bfloat16

GLA_HEADS = 4
GLA_DK = 64
GLA_DV = 128
GLA_KW = GLA_HEADS * GLA_DK
GLA_WIDTH = GLA_HEADS * GLA_DV
GLA_GATE_RANK = 16
GLA_TAU = 16.0
CONV_CH = 256
CONV_TAPS = 3
XA_HEADS = 4
XA_DH = 64
XA_WIDTH = XA_HEADS * XA_DH
LN_EPS = 1e-5
RMS_EPS = 1e-6
IN_SIZES = (GLA_KW, GLA_KW, GLA_WIDTH, GLA_WIDTH, GLA_GATE_RANK,
            CONV_CH, CONV_CH, CONV_CH, XA_WIDTH)

LANES = 128
SUBLANES = 8
VMEM_LIMIT_BYTES = 56 * 1024 * 1024

OFF_Q = 0
OFF_K = OFF_Q + GLA_KW
OFF_V = OFF_K + GLA_KW
OFF_G = OFF_V + GLA_WIDTH
OFF_CB = OFF_G + GLA_WIDTH
OFF_CC = OFF_CB + CONV_CH
OFF_CH = OFF_CC + CONV_CH
OFF_XQ = OFF_CH + CONV_CH
OFF_GATE = OFF_XQ + XA_WIDTH
IN_COLS_R = OFF_GATE + GLA_KW
GLA_COLS = OFF_CB

MIX_TM = 1024
GLA_C = 128
OUT_SUB = 2
FFN_TM = 1024
FFN_SUB = 4
SAFE_DECAY = 60.0

_NT = (((1,), (1,)), ((), ()))


def _layer_norm(r, g, b):
    mu = jnp.mean(r, axis=-1, keepdims=True)
    d = r - mu
    var = jnp.mean(d * d, axis=-1, keepdims=True)
    return d * lax.rsqrt(var + LN_EPS) * g + b


def _w_in_kernel(wt_ref, wg_ref, o_ref):
    a0, rank = OFF_CB, GLA_GATE_RANK
    for j in range(0, OFF_GATE, LANES):
        src = j if j < a0 else j + rank
        o_ref[:, j:j + LANES] = wt_ref[src:src + LANES, :].T.astype(BF16)
    pad_rows = LANES - rank
    wa_t = jnp.concatenate([wt_ref[a0:a0 + rank, :], jnp.zeros((pad_rows, wt_ref.shape[1]), F32)],
                           axis=0)
    wg = jnp.concatenate([wg_ref[...], jnp.zeros((pad_rows, GLA_KW), F32)], axis=0)
    gate_t = jnp.dot(wg.T, wa_t, preferred_element_type=F32,
                     precision=lax.Precision.HIGHEST)
    o_ref[:, OFF_GATE:OFF_GATE + GLA_KW] = gate_t.T.astype(BF16)


def _head_masks():
    lane = lax.broadcasted_iota(jnp.int32, (1, GLA_KW), 1)
    return [(lane // GLA_DK) == h for h in range(GLA_HEADS)]


def _stack_heads(q, head_mask):
    return jnp.concatenate([jnp.where(m, q, 0.0) for m in head_mask], axis=0).astype(BF16)


def _gla_head_out(scores_h, o_inter_h, v_h, g_h, gn, causal):
    a_h = jnp.where(causal, scores_h, 0.0).astype(BF16)
    o_h = jnp.dot(a_h, v_h, preferred_element_type=F32) + o_inter_h
    ms = jnp.mean(o_h * o_h, axis=-1, keepdims=True)
    o_n = o_h * lax.rsqrt(ms + RMS_EPS) * gn
    return (o_n * (g_h * jax.nn.sigmoid(g_h))).astype(BF16)


def _mixer_kernel(alpha, x_ref, win_ref, bg_ref, cw_ref, gn_ref, mem_ref, wkv_ref, wout_ref,
                  g1_ref, b1_ref, o_ref, z_ref, b_ref, sst_ref, y_ref, u_ref, s_ref, sc_ref,
                  mkv_ref):
    tm = x_ref.shape[1]
    c = GLA_C
    n_chunks = tm // c
    chunk_rows = [slice(ci * c, (ci + 1) * c) for ci in range(n_chunks)]
    head_rows = [slice(h * c, (h + 1) * c) for h in range(GLA_HEADS)]
    head_keys = [slice(h * GLA_DK, (h + 1) * GLA_DK) for h in range(GLA_HEADS)]
    head_vals = [slice(h * GLA_DV, (h + 1) * GLA_DV) for h in range(GLA_HEADS)]

    @pl.when(pl.program_id(1) == 0)
    def _start_of_sequence():
        s_ref[...] = jnp.zeros_like(s_ref)
        u_ref[0:SUBLANES, :] = jnp.zeros((SUBLANES, CONV_CH), F32)
        mkv_ref[...] = jnp.dot(mem_ref[0].astype(BF16), wkv_ref[...],
                               preferred_element_type=F32).astype(BF16)

    xb = x_ref[0].astype(BF16)

    def proj(off, width):
        return jnp.dot(xb, win_ref[:, off:off + width], preferred_element_type=F32)

    def v_of(rows, h):
        return z_ref[rows, OFF_V + h * GLA_DV:OFF_V + (h + 1) * GLA_DV].astype(BF16)

    head_mask = _head_masks()
    row_i = lax.broadcasted_iota(jnp.int32, (c, c), 0)
    col_i = lax.broadcasted_iota(jnp.int32, (c, c), 1)
    causal = row_i >= col_i
    tri = causal.astype(BF16)
    q_scale = GLA_DK ** -0.5
    gn = gn_ref[...]

    pre = proj(OFF_GATE, GLA_KW) + bg_ref[...]
    la = (jnp.minimum(pre, 0.0) - jnp.log1p(jnp.exp(-jnp.abs(pre)))) * (1.0 / GLA_TAU)
    z_ref[:, OFF_Q:OFF_Q + 2 * GLA_KW] = proj(OFF_Q, 2 * GLA_KW)
    for rows in chunk_rows:
        la_c = la[rows, :]
        hi = la_c.astype(BF16)
        lo = (la_c - hi.astype(F32)).astype(BF16)
        b_ref[rows, :] = (jnp.dot(tri, hi, preferred_element_type=F32)
                          + jnp.dot(tri, lo, preferred_element_type=F32))

    z_ref[:, OFF_V:OFF_V + 2 * GLA_WIDTH] = proj(OFF_V, 2 * GLA_WIDTH)
    xq = proj(OFF_XQ, XA_WIDTH) * (XA_DH ** -0.5)

    worst = jnp.zeros((1, GLA_KW), F32)
    qs_chunks, scores_chunks, dec_chunks, upd_chunks = [], [], [], []
    for rows in chunk_rows:
        b = b_ref[rows, :]
        b_last = b[c - 1:c, :]
        worst = jnp.maximum(worst, -b_last)
        kmat = z_ref[rows, OFF_K:OFF_K + GLA_KW]
        qs = _stack_heads(z_ref[rows, OFF_Q:OFF_Q + GLA_KW] * (jnp.exp(b) * q_scale),
                          head_mask)
        kt = (kmat * jnp.exp(-b)).astype(BF16)
        kd_t = (kmat * jnp.exp(b_last - b)).T.astype(BF16)
        qs_chunks.append(qs)
        scores_chunks.append(lax.dot_general(qs, kt, _NT, preferred_element_type=F32))
        dec_chunks.append(jnp.broadcast_to(jnp.exp(b_last), (GLA_DV, GLA_KW)).T)
        upd_chunks.append(jnp.concatenate(
            [jnp.dot(kd_t[head_keys[h], :], v_of(rows, h), preferred_element_type=F32)
             for h in range(GLA_HEADS)], axis=0))

    mk = mkv_ref[:, 0:XA_WIDTH]
    mv = mkv_ref[:, XA_WIDTH:2 * XA_WIDTH].astype(F32)
    probs = []
    for h in range(XA_HEADS):
        q_h = jnp.where(head_mask[h], xq, 0.0).astype(BF16)
        s = lax.dot_general(q_h, mk, _NT, preferred_element_type=F32)
        e = jnp.exp(s - jnp.max(s, axis=-1, keepdims=True))
        probs.append((e * (1.0 / jnp.sum(e, axis=-1, keepdims=True))).astype(BF16))


    s_c = s_ref[...]
    for ci, rows in enumerate(chunk_rows):
        sst_ref[ci] = s_c
        o_inter = jnp.dot(qs_chunks[ci], s_c.astype(BF16), preferred_element_type=F32)
        for h in range(GLA_HEADS):
            y_ref[rows, head_vals[h]] = _gla_head_out(
                scores_chunks[ci][head_rows[h], :], o_inter[head_rows[h], :],
                v_of(rows, h), z_ref[rows, OFF_G + h * GLA_DV:OFF_G + (h + 1) * GLA_DV],
                gn, causal)
        s_c = s_c * dec_chunks[ci] + upd_chunks[ci]
    s_ref[...] = s_c

    zc = proj(OFF_CB, 3 * CONV_CH)
    u = zc[:, CONV_CH:2 * CONV_CH] * zc[:, 2 * CONV_CH:]
    u_ref[SUBLANES:SUBLANES + tm, :] = u
    u_m1 = u_ref[SUBLANES - 1:SUBLANES - 1 + tm, :]
    u_m2 = u_ref[SUBLANES - 2:SUBLANES - 2 + tm, :]
    conv = cw_ref[0:1, :] * u_m2 + cw_ref[1:2, :] * u_m1 + cw_ref[2:3, :] * u
    y_conv = (zc[:, 0:CONV_CH] * conv).astype(BF16)
    y_ref[:, GLA_WIDTH:GLA_WIDTH + CONV_CH] = y_conv
    u_ref[0:SUBLANES, :] = u_ref[tm:tm + SUBLANES, :]

    mix_gla = [jnp.dot(y_ref[rows, 0:GLA_WIDTH], wout_ref[0:GLA_WIDTH, :],
                       preferred_element_type=F32) for rows in chunk_rows]

    xa = jnp.zeros((tm, XA_WIDTH), F32)
    for h in range(XA_HEADS):
        mv_h = jnp.where(head_mask[h], mv, 0.0).astype(BF16)
        xa = xa + jnp.dot(probs[h], mv_h, preferred_element_type=F32)
    y_xa = xa.astype(BF16)
    y_ref[:, GLA_WIDTH + CONV_CH:] = y_xa

    sub = tm // OUT_SUB
    for i in range(OUT_SUB):
        rows = slice(i * sub, (i + 1) * sub)
        mix = (jnp.concatenate(mix_gla[i * sub // c:(i + 1) * sub // c], axis=0)
               + jnp.dot(y_conv[rows, :], wout_ref[GLA_WIDTH:GLA_WIDTH + CONV_CH, :],
                         preferred_element_type=F32)
               + jnp.dot(y_xa[rows, :], wout_ref[GLA_WIDTH + CONV_CH:, :],
                         preferred_element_type=F32))
        o_ref[0, rows, :] = _layer_norm(alpha * x_ref[0, rows, :] + mix,
                                        g1_ref[...], b1_ref[...])

    @pl.when(jnp.max(worst) > SAFE_DECAY)
    def _redo_with_rowwise_scores():
        sub8 = lax.broadcasted_iota(jnp.int32, (SUBLANES, GLA_KW), 0)
        lane8 = lax.broadcasted_iota(jnp.int32, (SUBLANES, GLA_KW), 1)
        head_sel8 = ((lane8 // GLA_DK) == sub8).astype(F32)

        def chunk_body(ci, carry):
            r0 = pl.multiple_of(ci * c, c)
            rows = pl.ds(r0, c)
            b = b_ref[rows, :]
            kmat = z_ref[rows, OFF_K:OFF_K + GLA_KW]

            def row_body(i, carry2):
                b_i = b_ref[pl.ds(r0 + i, 1), :]
                q_i = z_ref[pl.ds(r0 + i, 1), OFF_Q:OFF_Q + GLA_KW] * q_scale
                w = (kmat * jnp.exp(jnp.minimum(b_i - b, 0.0))).astype(BF16)
                q8 = (jnp.broadcast_to(q_i, (SUBLANES, GLA_KW)) * head_sel8).astype(BF16)
                r = lax.dot_general(q8, w, _NT, preferred_element_type=F32)
                for h in range(GLA_HEADS):
                    sc_ref[pl.ds(h * c + i, 1), :] = r[h:h + 1, :]
                return carry2

            lax.fori_loop(0, c, row_body, 0)
            qs = _stack_heads(z_ref[rows, OFF_Q:OFF_Q + GLA_KW] * (jnp.exp(b) * q_scale),
                              head_mask)
            o_inter = jnp.dot(qs, sst_ref[ci].astype(BF16), preferred_element_type=F32)
            for h in range(GLA_HEADS):
                y_ref[rows, head_vals[h]] = _gla_head_out(
                    sc_ref[head_rows[h], :], o_inter[head_rows[h], :], v_of(rows, h),
                    z_ref[rows, OFF_G + h * GLA_DV:OFF_G + (h + 1) * GLA_DV], gn, causal)
            return carry

        lax.fori_loop(0, n_chunks, chunk_body, 0)
        mix = jnp.dot(y_ref[...], wout_ref[...], preferred_element_type=F32)
        o_ref[0] = _layer_norm(alpha * x_ref[0] + mix, g1_ref[...], b1_ref[...])


def _ffn_kernel(alpha, x_ref, w1_ref, w2_ref, g2_ref, b2_ref, o_ref):
    sub = x_ref.shape[0] // FFN_SUB
    for i in range(FFN_SUB):
        rows = slice(i * sub, (i + 1) * sub)
        x = x_ref[rows, :]
        h = jnp.dot(x.astype(BF16), w1_ref[...], preferred_element_type=F32)
        h = jnp.square(jnp.maximum(h, 0.0)).astype(BF16)
        ff = jnp.dot(h, w2_ref[...], preferred_element_type=F32)
        o_ref[rows, :] = _layer_norm(alpha * x + ff, g2_ref[...], b2_ref[...])


def _const_spec(shape):
    return pl.BlockSpec(shape, lambda *_: (0,) * len(shape))


def _prep_w_in(w_in_t, w_g):
    assert w_in_t.shape[0] == sum(IN_SIZES) and sum(IN_SIZES[:4]) == OFF_CB
    return pl.pallas_call(
        _w_in_kernel,
        out_shape=jax.ShapeDtypeStruct((w_in_t.shape[1], IN_COLS_R), BF16),
        compiler_params=pltpu.CompilerParams(vmem_limit_bytes=VMEM_LIMIT_BYTES),
        name="w_in_prep",
    )(w_in_t, w_g)


def _mixer(alpha, x, w_in_r, bg, cw, gn, mem, w_kv, w_out, g1, b1):
    bsz, s, d = x.shape
    m = mem.shape[1]
    tm = MIX_TM
    assert s % tm == 0 and tm % GLA_C == 0 and (tm // OUT_SUB) % GLA_C == 0
    return pl.pallas_call(
        functools.partial(_mixer_kernel, alpha),
        grid=(bsz, s // tm),
        in_specs=[
            pl.BlockSpec((1, tm, d), lambda b, j: (b, j, 0)),
            _const_spec(w_in_r.shape), _const_spec(bg.shape),
            _const_spec(cw.shape), _const_spec(gn.shape),
            pl.BlockSpec((1, m, d), lambda b, j: (b, 0, 0)),
            _const_spec(w_kv.shape), _const_spec(w_out.shape),
            _const_spec(g1.shape), _const_spec(b1.shape),
        ],
        out_specs=pl.BlockSpec((1, tm, d), lambda b, j: (b, j, 0)),
        out_shape=jax.ShapeDtypeStruct((bsz, s, d), F32),
        scratch_shapes=[
            pltpu.VMEM((tm, GLA_COLS), F32),
            pltpu.VMEM((tm, GLA_KW), F32),
            pltpu.VMEM((tm // GLA_C, GLA_KW, GLA_DV), F32),
            pltpu.VMEM((tm, d), BF16),
            pltpu.VMEM((tm + SUBLANES, CONV_CH), F32),
            pltpu.VMEM((GLA_KW, GLA_DV), F32),
            pltpu.VMEM((GLA_HEADS * GLA_C, GLA_C), F32),
            pltpu.VMEM((m, 2 * XA_WIDTH), BF16),
        ],
        compiler_params=pltpu.CompilerParams(
            dimension_semantics=("arbitrary", "arbitrary"),
            vmem_limit_bytes=VMEM_LIMIT_BYTES),
        name="mixer",
    )(x, w_in_r, bg, cw, gn, mem, w_kv, w_out, g1, b1)


def _ffn(alpha, x2d, w1, w2, g2, b2):
    t, d = x2d.shape
    tm = FFN_TM
    assert t % tm == 0 and tm % FFN_SUB == 0
    return pl.pallas_call(
        functools.partial(_ffn_kernel, alpha),
        grid=(t // tm,),
        in_specs=[
            pl.BlockSpec((tm, d), lambda i: (i, 0)),
            _const_spec(w1.shape), _const_spec(w2.shape),
            _const_spec(g2.shape), _const_spec(b2.shape),
        ],
        out_specs=pl.BlockSpec((tm, d), lambda i: (i, 0)),
        out_shape=jax.ShapeDtypeStruct((t, d), F32),
        compiler_params=pltpu.CompilerParams(
            dimension_semantics=("arbitrary",), vmem_limit_bytes=VMEM_LIMIT_BYTES),
        name="ffn",
    )(x2d, w1, w2, g2, b2)


def kernel(x, mem, w_in, w_gate_up, b_gate, conv_w, gla_norm_g, w_mem_kv, w_out,
           ln1_g, ln1_b, w_ff1, w_ff2, ln2_g, ln2_b):
    depth = w_in.shape[0]
    bsz, s, d = x.shape
    alpha = (2.0 * depth) ** 0.25
    for l in range(depth):
        w_in_r = _prep_w_in(w_in[l].T, w_gate_up[l])
        x = _mixer(alpha, x, w_in_r, b_gate[l][None, :], conv_w[l], gla_norm_g[l][None, :],
                   mem, w_mem_kv[l].astype(BF16), w_out[l].astype(BF16),
                   ln1_g[l][None, :], ln1_b[l][None, :])
        x = _ffn(alpha, x.reshape(bsz * s, d), w_ff1[l].astype(BF16), w_ff2[l].astype(BF16),
                 ln2_g[l][None, :], ln2_b[l][None, :]).reshape(bsz, s, d)
    return x
```

```python
import functools

import jax
import jax.numpy as jnp
from jax import lax
from jax.experimental import pallas as pl
from jax.experimental.pallas import tpu as pltpu

F32 = jnp.float32
BF16 = jnp.bfloat16

GLA_HEADS = 4
GLA_DK = 64
GLA_DV = 128
GLA_KW = GLA_HEADS * GLA_DK
GLA_WIDTH = GLA_HEADS * GLA_DV
GLA_GATE_RANK = 16
GLA_TAU = 16.0
CONV_CH = 256
CONV_TAPS = 3
XA_HEADS = 4
XA_DH = 64
XA_WIDTH = XA_HEADS * XA_DH
LN_EPS = 1e-5
RMS_EPS = 1e-6
IN_SIZES = (GLA_KW, GLA_KW, GLA_WIDTH, GLA_WIDTH, GLA_GATE_RANK,
            CONV_CH, CONV_CH, CONV_CH, XA_WIDTH)

LANES = 128
SUBLANES = 8
VMEM_LIMIT_BYTES = 56 * 1024 * 1024

OFF_Q = 0
OFF_K = OFF_Q + GLA_KW
OFF_V = OFF_K + GLA_KW
OFF_G = OFF_V + GLA_WIDTH
OFF_CB = OFF_G + GLA_WIDTH
OFF_CC = OFF_CB + CONV_CH
OFF_CH = OFF_CC + CONV_CH
OFF_XQ = OFF_CH + CONV_CH
OFF_GATE = OFF_XQ + XA_WIDTH
IN_COLS_R = OFF_GATE + GLA_KW
GLA_COLS = OFF_CB

MIX_TM = 1024
GLA_C = 128
OUT_SUB = 4
FFN_TM = 1024
FFN_SUB = 4
SAFE_DECAY = 60.0

_NT = (((1,), (1,)), ((), ()))


def _layer_norm(r, g, b):
    mu = jnp.mean(r, axis=-1, keepdims=True)
    d = r - mu
    var = jnp.mean(d * d, axis=-1, keepdims=True)
    return d * lax.rsqrt(var + LN_EPS) * g + b


def _w_in_kernel(wt_ref, wg_ref, o_ref):
    a0, rank = OFF_CB, GLA_GATE_RANK
    for j in range(0, OFF_GATE, LANES):
        src = j if j < a0 else j + rank
        o_ref[:, j:j + LANES] = wt_ref[src:src + LANES, :].T.astype(BF16)
    pad_rows = LANES - rank
    wa_t = jnp.concatenate([wt_ref[a0:a0 + rank, :], jnp.zeros((pad_rows, wt_ref.shape[1]), F32)],
                           axis=0)
    wg = jnp.concatenate([wg_ref[...], jnp.zeros((pad_rows, GLA_KW), F32)], axis=0)
    gate_t = jnp.dot(wg.T, wa_t, preferred_element_type=F32,
                     precision=lax.Precision.HIGHEST)
    o_ref[:, OFF_GATE:OFF_GATE + GLA_KW] = gate_t.T.astype(BF16)


def _head_masks():
    lane = lax.broadcasted_iota(jnp.int32, (1, GLA_KW), 1)
    return [(lane // GLA_DK) == h for h in range(GLA_HEADS)]


def _stack_heads(q, head_mask):
    return jnp.concatenate([jnp.where(m, q, 0.0) for m in head_mask], axis=0).astype(BF16)


def _gla_head_out(scores_h, o_inter_h, v_h, g_h, gn, causal):
    a_h = jnp.where(causal, scores_h, 0.0).astype(BF16)
    o_h = jnp.dot(a_h, v_h, preferred_element_type=F32) + o_inter_h
    ms = jnp.mean(o_h * o_h, axis=-1, keepdims=True)
    o_n = o_h * lax.rsqrt(ms + RMS_EPS) * gn
    return (o_n * (g_h * jax.nn.sigmoid(g_h))).astype(BF16)


def _mixer_kernel(alpha, x_ref, win_ref, bg_ref, cw_ref, gn_ref, mem_ref, wkv_ref, wout_ref,
                  g1_ref, b1_ref, o_ref, z_ref, b_ref, sst_ref, y_ref, u_ref, s_ref, sc_ref,
                  mkv_ref):
    tm = x_ref.shape[1]
    c = GLA_C
    n_chunks = tm // c
    chunk_rows = [slice(ci * c, (ci + 1) * c) for ci in range(n_chunks)]
    head_rows = [slice(h * c, (h + 1) * c) for h in range(GLA_HEADS)]
    head_keys = [slice(h * GLA_DK, (h + 1) * GLA_DK) for h in range(GLA_HEADS)]
    head_vals = [slice(h * GLA_DV, (h + 1) * GLA_DV) for h in range(GLA_HEADS)]

    @pl.when(pl.program_id(1) == 0)
    def _start_of_sequence():
        s_ref[...] = jnp.zeros_like(s_ref)
        u_ref[0:SUBLANES, :] = jnp.zeros((SUBLANES, CONV_CH), F32)
        mkv_ref[...] = jnp.dot(mem_ref[0].astype(BF16), wkv_ref[...],
                               preferred_element_type=F32).astype(BF16)

    xb = x_ref[0].astype(BF16)

    def proj(off, width):
        return jnp.dot(xb, win_ref[:, off:off + width], preferred_element_type=F32)

    def v_of(rows, h):
        return z_ref[rows, OFF_V + h * GLA_DV:OFF_V + (h + 1) * GLA_DV].astype(BF16)

    head_mask = _head_masks()
    row_i = lax.broadcasted_iota(jnp.int32, (c, c), 0)
    col_i = lax.broadcasted_iota(jnp.int32, (c, c), 1)
    causal = row_i >= col_i
    tri = causal.astype(BF16)
    q_scale = GLA_DK ** -0.5
    gn = gn_ref[...]

    pre = proj(OFF_GATE, GLA_KW) + bg_ref[...]
    la = (jnp.minimum(pre, 0.0) - jnp.log(1.0 + jnp.exp(-jnp.abs(pre)))) * (1.0 / GLA_TAU)
    z_ref[:, OFF_Q:OFF_Q + 2 * GLA_KW] = proj(OFF_Q, 2 * GLA_KW)
    for rows in chunk_rows:
        la_c = la[rows, :]
        hi = la_c.astype(BF16)
        lo = (la_c - hi.astype(F32)).astype(BF16)
        b_ref[rows, :] = (jnp.dot(tri, hi, preferred_element_type=F32)
                          + jnp.dot(tri, lo, preferred_element_type=F32))

    z_ref[:, OFF_V:OFF_V + 2 * GLA_WIDTH] = proj(OFF_V, 2 * GLA_WIDTH)
    xq = proj(OFF_XQ, XA_WIDTH) * (XA_DH ** -0.5)

    worst = jnp.zeros((1, GLA_KW), F32)
    qs_chunks, scores_chunks, dec_chunks, upd_chunks = [], [], [], []
    for rows in chunk_rows:
        b = b_ref[rows, :]
        b_last = b[c - 1:c, :]
        worst = jnp.maximum(worst, -b_last)
        kmat = z_ref[rows, OFF_K:OFF_K + GLA_KW]
        qs = _stack_heads(z_ref[rows, OFF_Q:OFF_Q + GLA_KW] * (jnp.exp(b) * q_scale),
                          head_mask)
        kt = (kmat * jnp.exp(-b)).astype(BF16)
        kd_t = (kmat * jnp.exp(b_last - b)).T.astype(BF16)
        qs_chunks.append(qs)
        scores_chunks.append(lax.dot_general(qs, kt, _NT, preferred_element_type=F32))
        dec_chunks.append(jnp.broadcast_to(jnp.exp(b_last), (GLA_DV, GLA_KW)).T)
        upd_chunks.append(jnp.concatenate(
            [jnp.dot(kd_t[head_keys[h], :], v_of(rows, h), preferred_element_type=F32)
             for h in range(GLA_HEADS)], axis=0))

    mk = mkv_ref[:, 0:XA_WIDTH]
    mv = mkv_ref[:, XA_WIDTH:2 * XA_WIDTH].astype(F32)
    probs = []
    for h in range(XA_HEADS):
        q_h = jnp.where(head_mask[h], xq, 0.0).astype(BF16)
        s = lax.dot_general(q_h, mk, _NT, preferred_element_type=F32)
        e = jnp.exp(s - jnp.max(s, axis=-1, keepdims=True))
        probs.append((e * (1.0 / jnp.sum(e, axis=-1, keepdims=True))).astype(BF16))


    s_c = s_ref[...]
    for ci, rows in enumerate(chunk_rows):
        sst_ref[ci] = s_c
        o_inter = jnp.dot(qs_chunks[ci], s_c.astype(BF16), preferred_element_type=F32)
        for h in range(GLA_HEADS):
            y_ref[rows, head_vals[h]] = _gla_head_out(
                scores_chunks[ci][head_rows[h], :], o_inter[head_rows[h], :],
                v_of(rows, h), z_ref[rows, OFF_G + h * GLA_DV:OFF_G + (h + 1) * GLA_DV],
                gn, causal)
        s_c = s_c * dec_chunks[ci] + upd_chunks[ci]
    s_ref[...] = s_c

    zc = proj(OFF_CB, 3 * CONV_CH)
    u = zc[:, CONV_CH:2 * CONV_CH] * zc[:, 2 * CONV_CH:]
    u_ref[SUBLANES:SUBLANES + tm, :] = u
    u_m1 = u_ref[SUBLANES - 1:SUBLANES - 1 + tm, :]
    u_m2 = u_ref[SUBLANES - 2:SUBLANES - 2 + tm, :]
    conv = cw_ref[0:1, :] * u_m2 + cw_ref[1:2, :] * u_m1 + cw_ref[2:3, :] * u
    y_conv = (zc[:, 0:CONV_CH] * conv).astype(BF16)
    y_ref[:, GLA_WIDTH:GLA_WIDTH + CONV_CH] = y_conv
    u_ref[0:SUBLANES, :] = u_ref[tm:tm + SUBLANES, :]

    xa = jnp.zeros((tm, XA_WIDTH), F32)
    for h in range(XA_HEADS):
        mv_h = jnp.where(head_mask[h], mv, 0.0).astype(BF16)
        xa = xa + jnp.dot(probs[h], mv_h, preferred_element_type=F32)
    y_xa = xa.astype(BF16)
    y_ref[:, GLA_WIDTH + CONV_CH:] = y_xa

    sub = tm // OUT_SUB
    for i in range(OUT_SUB):
        rows = slice(i * sub, (i + 1) * sub)
        mix = jnp.dot(y_ref[rows, :], wout_ref[...], preferred_element_type=F32)
        o_ref[0, rows, :] = _layer_norm(alpha * x_ref[0, rows, :] + mix,
                                        g1_ref[...], b1_ref[...])

    @pl.when(jnp.max(worst) > SAFE_DECAY)
    def _redo_with_rowwise_scores():
        sub8 = lax.broadcasted_iota(jnp.int32, (SUBLANES, GLA_KW), 0)
        lane8 = lax.broadcasted_iota(jnp.int32, (SUBLANES, GLA_KW), 1)
        head_sel8 = ((lane8 // GLA_DK) == sub8).astype(F32)

        def chunk_body(ci, carry):
            r0 = pl.multiple_of(ci * c, c)
            rows = pl.ds(r0, c)
            b = b_ref[rows, :]
            kmat = z_ref[rows, OFF_K:OFF_K + GLA_KW]

            def row_body(i, carry2):
                b_i = b_ref[pl.ds(r0 + i, 1), :]
                q_i = z_ref[pl.ds(r0 + i, 1), OFF_Q:OFF_Q + GLA_KW] * q_scale
                w = (kmat * jnp.exp(jnp.minimum(b_i - b, 0.0))).astype(BF16)
                q8 = (jnp.broadcast_to(q_i, (SUBLANES, GLA_KW)) * head_sel8).astype(BF16)
                r = lax.dot_general(q8, w, _NT, preferred_element_type=F32)
                for h in range(GLA_HEADS):
                    sc_ref[pl.ds(h * c + i, 1), :] = r[h:h + 1, :]
                return carry2

            lax.fori_loop(0, c, row_body, 0)
            qs = _stack_heads(z_ref[rows, OFF_Q:OFF_Q + GLA_KW] * (jnp.exp(b) * q_scale),
                              head_mask)
            o_inter = jnp.dot(qs, sst_ref[ci].astype(BF16), preferred_element_type=F32)
            for h in range(GLA_HEADS):
                y_ref[rows, head_vals[h]] = _gla_head_out(
                    sc_ref[head_rows[h], :], o_inter[head_rows[h], :], v_of(rows, h),
                    z_ref[rows, OFF_G + h * GLA_DV:OFF_G + (h + 1) * GLA_DV], gn, causal)
            return carry

        lax.fori_loop(0, n_chunks, chunk_body, 0)
        mix = jnp.dot(y_ref[...], wout_ref[...], preferred_element_type=F32)
        o_ref[0] = _layer_norm(alpha * x_ref[0] + mix, g1_ref[...], b1_ref[...])


def _ffn_kernel(alpha, x_ref, w1_ref, w2_ref, g2_ref, b2_ref, o_ref):
    sub = x_ref.shape[0] // FFN_SUB
    for i in range(FFN_SUB):
        rows = slice(i * sub, (i + 1) * sub)
        x = x_ref[rows, :]
        h = jnp.dot(x.astype(BF16), w1_ref[...], preferred_element_type=F32)
        h = jnp.square(jnp.maximum(h, 0.0)).astype(BF16)
        ff = jnp.dot(h, w2_ref[...], preferred_element_type=F32)
        o_ref[rows, :] = _layer_norm(alpha * x + ff, g2_ref[...], b2_ref[...])


def _const_spec(shape):
    return pl.BlockSpec(shape, lambda *_: (0,) * len(shape))


def _prep_w_in(w_in_t, w_g):
    assert w_in_t.shape[0] == sum(IN_SIZES) and sum(IN_SIZES[:4]) == OFF_CB
    return pl.pallas_call(
        _w_in_kernel,
        out_shape=jax.ShapeDtypeStruct((w_in_t.shape[1], IN_COLS_R), BF16),
        compiler_params=pltpu.CompilerParams(vmem_limit_bytes=VMEM_LIMIT_BYTES),
        name="w_in_prep",
    )(w_in_t, w_g)


def _mixer(alpha, x, w_in_r, bg, cw, gn, mem, w_kv, w_out, g1, b1):
    bsz, s, d = x.shape
    m = mem.shape[1]
    tm = MIX_TM
    assert s % tm == 0 and tm % GLA_C == 0 and (tm // OUT_SUB) % GLA_C == 0
    return pl.pallas_call(
        functools.partial(_mixer_kernel, alpha),
        grid=(bsz, s // tm),
        in_specs=[
            pl.BlockSpec((1, tm, d), lambda b, j: (b, j, 0)),
            _const_spec(w_in_r.shape), _const_spec(bg.shape),
            _const_spec(cw.shape), _const_spec(gn.shape),
            pl.BlockSpec((1, m, d), lambda b, j: (b, 0, 0)),
            _const_spec(w_kv.shape), _const_spec(w_out.shape),
            _const_spec(g1.shape), _const_spec(b1.shape),
        ],
        out_specs=pl.BlockSpec((1, tm, d), lambda b, j: (b, j, 0)),
        out_shape=jax.ShapeDtypeStruct((bsz, s, d), F32),
        scratch_shapes=[
            pltpu.VMEM((tm, GLA_COLS), F32),
            pltpu.VMEM((tm, GLA_KW), F32),
            pltpu.VMEM((tm // GLA_C, GLA_KW, GLA_DV), F32),
            pltpu.VMEM((tm, d), BF16),
            pltpu.VMEM((tm + SUBLANES, CONV_CH), F32),
            pltpu.VMEM((GLA_KW, GLA_DV), F32),
            pltpu.VMEM((GLA_HEADS * GLA_C, GLA_C), F32),
            pltpu.VMEM((m, 2 * XA_WIDTH), BF16),
        ],
        compiler_params=pltpu.CompilerParams(
            dimension_semantics=("arbitrary", "arbitrary"),
            vmem_limit_bytes=VMEM_LIMIT_BYTES),
        name="mixer",
    )(x, w_in_r, bg, cw, gn, mem, w_kv, w_out, g1, b1)


def _ffn(alpha, x2d, w1, w2, g2, b2):
    t, d = x2d.shape
    tm = FFN_TM
    assert t % tm == 0 and tm % FFN_SUB == 0
    return pl.pallas_call(
        functools.partial(_ffn_kernel, alpha),
        grid=(t // tm,),
        in_specs=[
            pl.BlockSpec((tm, d), lambda i: (i, 0)),
            _const_spec(w1.shape), _const_spec(w2.shape),
            _const_spec(g2.shape), _const_spec(b2.shape),
        ],
        out_specs=pl.BlockSpec((tm, d), lambda i: (i, 0)),
        out_shape=jax.ShapeDtypeStruct((t, d), F32),
        compiler_params=pltpu.CompilerParams(
            dimension_semantics=("arbitrary",), vmem_limit_bytes=VMEM_LIMIT_BYTES),
        name="ffn",
    )(x2d, w1, w2, g2, b2)


def kernel(x, mem, w_in, w_gate_up, b_gate, conv_w, gla_norm_g, w_mem_kv, w_out,
           ln1_g, ln1_b, w_ff1, w_ff2, ln2_g, ln2_b):
    depth = w_in.shape[0]
    bsz, s, d = x.shape
    alpha = (2.0 * depth) ** 0.25
    for l in range(depth):
        w_in_r = _prep_w_in(w_in[l].T, w_gate_up[l])
        x = _mixer(alpha, x, w_in_r, b_gate[l][None, :], conv_w[l], gla_norm_g[l][None, :],
                   mem, w_mem_kv[l].astype(BF16), w_out[l].astype(BF16),
                   ln1_g[l][None, :], ln1_b[l][None, :])
        x = _ffn(alpha, x.reshape(bsz * s, d), w_ff1[l].astype(BF16), w_ff2[l].astype(BF16),
                 ln2_g[l][None, :], ln2_b[l][None, :]).reshape(bsz, s, d)
    return x
```

```python
import functools

import jax
import jax.numpy as jnp
from jax import lax
from jax.experimental import pallas as pl
from jax.experimental.pallas import tpu as pltpu

F32 = jnp.float32
BF16 = jnp.bfloat16

GLA_HEADS = 4
GLA_DK = 64
GLA_DV = 128
GLA_KW = GLA_HEADS * GLA_DK
GLA_WIDTH = GLA_HEADS * GLA_DV
GLA_GATE_RANK = 16
GLA_TAU = 16.0
CONV_CH = 256
CONV_TAPS = 3
XA_HEADS = 4
XA_DH = 64
XA_WIDTH = XA_HEADS * XA_DH
LN_EPS = 1e-5
RMS_EPS = 1e-6
IN_SIZES = (GLA_KW, GLA_KW, GLA_WIDTH, GLA_WIDTH, GLA_GATE_RANK,
            CONV_CH, CONV_CH, CONV_CH, XA_WIDTH)

LANES = 128
SUBLANES = 8
VMEM_LIMIT_BYTES = 56 * 1024 * 1024

OFF_Q = 0
OFF_K = OFF_Q + GLA_KW
OFF_V = OFF_K + GLA_KW
OFF_G = OFF_V + GLA_WIDTH
OFF_CB = OFF_G + GLA_WIDTH
OFF_CC = OFF_CB + CONV_CH
OFF_CH = OFF_CC + CONV_CH
OFF_XQ = OFF_CH + CONV_CH
OFF_GATE = OFF_XQ + XA_WIDTH
IN_COLS_R = OFF_GATE + GLA_KW
GLA_COLS = OFF_CB

MIX_TM = 1024
GLA_C = 128
OUT_SUB = 4
FFN_TM = 2048
FFN_SUB = 8
SAFE_DECAY = 60.0

_NT = (((1,), (1,)), ((), ()))


def _pack_rows(w):
    k, n = w.shape
    pairs = w.astype(BF16).reshape(k // 2, 2, n).swapaxes(-1, -2)
    return lax.bitcast_convert_type(pairs, jnp.uint32)


def _unpack_rows(w_ref):
    return pltpu.bitcast(w_ref[...], BF16)


def _layer_norm(r, g, b):
    mu = jnp.mean(r, axis=-1, keepdims=True)
    d = r - mu
    var = jnp.mean(d * d, axis=-1, keepdims=True)
    return d * lax.rsqrt(var + LN_EPS) * g + b


def _w_in_kernel(wt_ref, wg_ref, o_ref):
    a0, rank = OFF_CB, GLA_GATE_RANK
    for j in range(0, OFF_GATE, LANES):
        src = j if j < a0 else j + rank
        o_ref[:, j:j + LANES] = wt_ref[src:src + LANES, :].T.astype(BF16)
    pad_rows = LANES - rank
    wa_t = jnp.concatenate([wt_ref[a0:a0 + rank, :], jnp.zeros((pad_rows, wt_ref.shape[1]), F32)],
                           axis=0)
    wg = jnp.concatenate([wg_ref[...], jnp.zeros((pad_rows, GLA_KW), F32)], axis=0)
    gate_t = jnp.dot(wg.T, wa_t, preferred_element_type=F32,
                     precision=lax.Precision.HIGHEST)
    o_ref[:, OFF_GATE:OFF_GATE + GLA_KW] = gate_t.T.astype(BF16)


def _head_masks():
    lane = lax.broadcasted_iota(jnp.int32, (1, GLA_KW), 1)
    return [(lane // GLA_DK) == h for h in range(GLA_HEADS)]


def _stack_heads(q, head_mask):
    return jnp.concatenate([jnp.where(m, q, 0.0) for m in head_mask], axis=0).astype(BF16)


def _gla_head_out(scores_h, o_inter_h, v_h, g_h, gn, causal):
    a_h = jnp.where(causal, scores_h, 0.0).astype(BF16)
    o_h = jnp.dot(a_h, v_h, preferred_element_type=F32) + o_inter_h
    ms = jnp.mean(o_h * o_h, axis=-1, keepdims=True)
    o_n = o_h * lax.rsqrt(ms + RMS_EPS) * gn
    return (o_n * (g_h * jax.nn.sigmoid(g_h))).astype(BF16)


def _mixer_kernel(alpha, x_ref, win_ref, bg_ref, cw_ref, gn_ref, mem_ref, wkv_ref, wout_ref,
                  g1_ref, b1_ref, o_ref, z_ref, b_ref, sst_ref, y_ref, u_ref, s_ref, sc_ref,
                  mkv_ref):
    tm = x_ref.shape[1]
    c = GLA_C
    n_chunks = tm // c
    chunk_rows = [slice(ci * c, (ci + 1) * c) for ci in range(n_chunks)]
    head_rows = [slice(h * c, (h + 1) * c) for h in range(GLA_HEADS)]
    head_keys = [slice(h * GLA_DK, (h + 1) * GLA_DK) for h in range(GLA_HEADS)]
    head_vals = [slice(h * GLA_DV, (h + 1) * GLA_DV) for h in range(GLA_HEADS)]

    @pl.when(pl.program_id(1) == 0)
    def _start_of_sequence():
        s_ref[...] = jnp.zeros_like(s_ref)
        u_ref[0:SUBLANES, :] = jnp.zeros((SUBLANES, CONV_CH), F32)
        mkv_ref[...] = jnp.dot(mem_ref[0].astype(BF16), wkv_ref[...],
                               preferred_element_type=F32).astype(BF16)

    xb = x_ref[0].astype(BF16)

    def proj(off, width):
        return jnp.dot(xb, win_ref[:, off:off + width], preferred_element_type=F32)

    def v_of(rows, h):
        return z_ref[rows, OFF_V + h * GLA_DV:OFF_V + (h + 1) * GLA_DV].astype(BF16)

    head_mask = _head_masks()
    row_i = lax.broadcasted_iota(jnp.int32, (c, c), 0)
    col_i = lax.broadcasted_iota(jnp.int32, (c, c), 1)
    causal = row_i >= col_i
    tri = causal.astype(BF16)
    q_scale = GLA_DK ** -0.5
    gn = gn_ref[...]

    pre = proj(OFF_GATE, GLA_KW) + bg_ref[...]
    la = (jnp.minimum(pre, 0.0) - jnp.log(1.0 + jnp.exp(-jnp.abs(pre)))) * (1.0 / GLA_TAU)
    z_ref[:, OFF_Q:OFF_Q + 2 * GLA_KW] = proj(OFF_Q, 2 * GLA_KW)
    for rows in chunk_rows:
        la_c = la[rows, :]
        hi = la_c.astype(BF16)
        lo = (la_c - hi.astype(F32)).astype(BF16)
        b_ref[rows, :] = (jnp.dot(tri, hi, preferred_element_type=F32)
                          + jnp.dot(tri, lo, preferred_element_type=F32))

    z_ref[:, OFF_V:OFF_V + 2 * GLA_WIDTH] = proj(OFF_V, 2 * GLA_WIDTH)
    xq = proj(OFF_XQ, XA_WIDTH) * (XA_DH ** -0.5)

    worst = jnp.zeros((1, GLA_KW), F32)
    qs_chunks, scores_chunks, dec_chunks, upd_chunks = [], [], [], []
    for rows in chunk_rows:
        b = b_ref[rows, :]
        b_last = b[c - 1:c, :]
        worst = jnp.maximum(worst, -b_last)
        kmat = z_ref[rows, OFF_K:OFF_K + GLA_KW]
        qs = _stack_heads(z_ref[rows, OFF_Q:OFF_Q + GLA_KW] * (jnp.exp(b) * q_scale),
                          head_mask)
        kt = (kmat * jnp.exp(-b)).astype(BF16)
        kd_t = (kmat * jnp.exp(b_last - b)).T.astype(BF16)
        qs_chunks.append(qs)
        scores_chunks.append(lax.dot_general(qs, kt, _NT, preferred_element_type=F32))
        dec_chunks.append(jnp.broadcast_to(jnp.exp(b_last), (GLA_DV, GLA_KW)).T)
        upd_chunks.append(jnp.concatenate(
            [jnp.dot(kd_t[head_keys[h], :], v_of(rows, h), preferred_element_type=F32)
             for h in range(GLA_HEADS)], axis=0))

    mk = mkv_ref[:, 0:XA_WIDTH]
    mv = mkv_ref[:, XA_WIDTH:2 * XA_WIDTH].astype(F32)
    probs = []
    for h in range(XA_HEADS):
        q_h = jnp.where(head_mask[h], xq, 0.0).astype(BF16)
        s = lax.dot_general(q_h, mk, _NT, preferred_element_type=F32)
        e = jnp.exp(s - jnp.max(s, axis=-1, keepdims=True))
        probs.append((e * (1.0 / jnp.sum(e, axis=-1, keepdims=True))).astype(BF16))


    s_c = s_ref[...]
    for ci, rows in enumerate(chunk_rows):
        sst_ref[ci] = s_c
        o_inter = jnp.dot(qs_chunks[ci], s_c.astype(BF16), preferred_element_type=F32)
        for h in range(GLA_HEADS):
            y_ref[rows, head_vals[h]] = _gla_head_out(
                scores_chunks[ci][head_rows[h], :], o_inter[head_rows[h], :],
                v_of(rows, h), z_ref[rows, OFF_G + h * GLA_DV:OFF_G + (h + 1) * GLA_DV],
                gn, causal)
        s_c = s_c * dec_chunks[ci] + upd_chunks[ci]
    s_ref[...] = s_c

    zc = proj(OFF_CB, 3 * CONV_CH)
    u = zc[:, CONV_CH:2 * CONV_CH] * zc[:, 2 * CONV_CH:]
    u_ref[SUBLANES:SUBLANES + tm, :] = u
    u_m1 = u_ref[SUBLANES - 1:SUBLANES - 1 + tm, :]
    u_m2 = u_ref[SUBLANES - 2:SUBLANES - 2 + tm, :]
    conv = cw_ref[0:1, :] * u_m2 + cw_ref[1:2, :] * u_m1 + cw_ref[2:3, :] * u
    y_conv = (zc[:, 0:CONV_CH] * conv).astype(BF16)
    y_ref[:, GLA_WIDTH:GLA_WIDTH + CONV_CH] = y_conv
    u_ref[0:SUBLANES, :] = u_ref[tm:tm + SUBLANES, :]

    xa = jnp.zeros((tm, XA_WIDTH), F32)
    for h in range(XA_HEADS):
        mv_h = jnp.where(head_mask[h], mv, 0.0).astype(BF16)
        xa = xa + jnp.dot(probs[h], mv_h, preferred_element_type=F32)
    y_xa = xa.astype(BF16)
    y_ref[:, GLA_WIDTH + CONV_CH:] = y_xa

    sub = tm // OUT_SUB
    for i in range(OUT_SUB):
        rows = slice(i * sub, (i + 1) * sub)
        mix = jnp.dot(y_ref[rows, :], wout_ref[...], preferred_element_type=F32)
        o_ref[0, rows, :] = _layer_norm(alpha * x_ref[0, rows, :] + mix,
                                        g1_ref[...], b1_ref[...])

    @pl.when(jnp.max(worst) > SAFE_DECAY)
    def _redo_with_rowwise_scores():
        sub8 = lax.broadcasted_iota(jnp.int32, (SUBLANES, GLA_KW), 0)
        lane8 = lax.broadcasted_iota(jnp.int32, (SUBLANES, GLA_KW), 1)
        head_sel8 = ((lane8 // GLA_DK) == sub8).astype(F32)

        def chunk_body(ci, carry):
            r0 = pl.multiple_of(ci * c, c)
            rows = pl.ds(r0, c)
            b = b_ref[rows, :]
            kmat = z_ref[rows, OFF_K:OFF_K + GLA_KW]

            def row_body(i, carry2):
                b_i = b_ref[pl.ds(r0 + i, 1), :]
                q_i = z_ref[pl.ds(r0 + i, 1), OFF_Q:OFF_Q + GLA_KW] * q_scale
                w = (kmat * jnp.exp(jnp.minimum(b_i - b, 0.0))).astype(BF16)
                q8 = (jnp.broadcast_to(q_i, (SUBLANES, GLA_KW)) * head_sel8).astype(BF16)
                r = lax.dot_general(q8, w, _NT, preferred_element_type=F32)
                for h in range(GLA_HEADS):
                    sc_ref[pl.ds(h * c + i, 1), :] = r[h:h + 1, :]
                return carry2

            lax.fori_loop(0, c, row_body, 0)
            qs = _stack_heads(z_ref[rows, OFF_Q:OFF_Q + GLA_KW] * (jnp.exp(b) * q_scale),
                              head_mask)
            o_inter = jnp.dot(qs, sst_ref[ci].astype(BF16), preferred_element_type=F32)
            for h in range(GLA_HEADS):
                y_ref[rows, head_vals[h]] = _gla_head_out(
                    sc_ref[head_rows[h], :], o_inter[head_rows[h], :], v_of(rows, h),
                    z_ref[rows, OFF_G + h * GLA_DV:OFF_G + (h + 1) * GLA_DV], gn, causal)
            return carry

        lax.fori_loop(0, n_chunks, chunk_body, 0)
        mix = jnp.dot(y_ref[...], wout_ref[...], preferred_element_type=F32)
        o_ref[0] = _layer_norm(alpha * x_ref[0] + mix, g1_ref[...], b1_ref[...])


def _ffn_kernel(alpha, x_ref, w1_ref, w2_ref, g2_ref, b2_ref, o_ref):
    sub = x_ref.shape[0] // FFN_SUB
    for i in range(FFN_SUB):
        rows = slice(i * sub, (i + 1) * sub)
        x = x_ref[rows, :]
        h = jnp.dot(x.astype(BF16), _unpack_rows(w1_ref), preferred_element_type=F32)
        h = jnp.square(jnp.maximum(h, 0.0)).astype(BF16)
        ff = jnp.dot(h, _unpack_rows(w2_ref), preferred_element_type=F32)
        o_ref[rows, :] = _layer_norm(alpha * x + ff, g2_ref[...], b2_ref[...])


def _const_spec(shape):
    return pl.BlockSpec(shape, lambda *_: (0,) * len(shape))


def _prep_w_in(w_in_t, w_g):
    assert w_in_t.shape[0] == sum(IN_SIZES) and sum(IN_SIZES[:4]) == OFF_CB
    return pl.pallas_call(
        _w_in_kernel,
        out_shape=jax.ShapeDtypeStruct((w_in_t.shape[1], IN_COLS_R), BF16),
        compiler_params=pltpu.CompilerParams(vmem_limit_bytes=VMEM_LIMIT_BYTES),
        name="w_in_prep",
    )(w_in_t, w_g)


def _mixer(alpha, x, w_in_r, bg, cw, gn, mem, w_kv, w_out, g1, b1):
    bsz, s, d = x.shape
    m = mem.shape[1]
    tm = MIX_TM
    assert s % tm == 0 and tm % GLA_C == 0 and (tm // OUT_SUB) % GLA_C == 0
    return pl.pallas_call(
        functools.partial(_mixer_kernel, alpha),
        grid=(bsz, s // tm),
        in_specs=[
            pl.BlockSpec((1, tm, d), lambda b, j: (b, j, 0)),
            _const_spec(w_in_r.shape), _const_spec(bg.shape),
            _const_spec(cw.shape), _const_spec(gn.shape),
            pl.BlockSpec((1, m, d), lambda b, j: (b, 0, 0)),
            _const_spec(w_kv.shape), _const_spec(w_out.shape),
            _const_spec(g1.shape), _const_spec(b1.shape),
        ],
        out_specs=pl.BlockSpec((1, tm, d), lambda b, j: (b, j, 0)),
        out_shape=jax.ShapeDtypeStruct((bsz, s, d), F32),
        scratch_shapes=[
            pltpu.VMEM((tm, GLA_COLS), F32),
            pltpu.VMEM((tm, GLA_KW), F32),
            pltpu.VMEM((tm // GLA_C, GLA_KW, GLA_DV), F32),
            pltpu.VMEM((tm, d), BF16),
            pltpu.VMEM((tm + SUBLANES, CONV_CH), F32),
            pltpu.VMEM((GLA_KW, GLA_DV), F32),
            pltpu.VMEM((GLA_HEADS * GLA_C, GLA_C), F32),
            pltpu.VMEM((m, 2 * XA_WIDTH), BF16),
        ],
        compiler_params=pltpu.CompilerParams(
            dimension_semantics=("arbitrary", "arbitrary"),
            vmem_limit_bytes=VMEM_LIMIT_BYTES),
        name="mixer",
    )(x, w_in_r, bg, cw, gn, mem, w_kv, w_out, g1, b1)


def _ffn(alpha, x2d, w1, w2, g2, b2):
    t, d = x2d.shape
    tm = FFN_TM
    assert t % tm == 0 and tm % FFN_SUB == 0
    return pl.pallas_call(
        functools.partial(_ffn_kernel, alpha),
        grid=(t // tm,),
        in_specs=[
            pl.BlockSpec((tm, d), lambda i: (i, 0)),
            _const_spec(w1.shape), _const_spec(w2.shape),
            _const_spec(g2.shape), _const_spec(b2.shape),
        ],
        out_specs=pl.BlockSpec((tm, d), lambda i: (i, 0)),
        out_shape=jax.ShapeDtypeStruct((t, d), F32),
        compiler_params=pltpu.CompilerParams(
            dimension_semantics=("arbitrary",), vmem_limit_bytes=VMEM_LIMIT_BYTES),
        name="ffn",
    )(x2d, w1, w2, g2, b2)


def kernel(x, mem, w_in, w_gate_up, b_gate, conv_w, gla_norm_g, w_mem_kv, w_out,
           ln1_g, ln1_b, w_ff1, w_ff2, ln2_g, ln2_b):
    depth = w_in.shape[0]
    bsz, s, d = x.shape
    alpha = (2.0 * depth) ** 0.25
    for l in range(depth):
        w_in_r = _prep_w_in(w_in[l].T, w_gate_up[l])
        x = _mixer(alpha, x, w_in_r, b_gate[l][None, :], conv_w[l], gla_norm_g[l][None, :],
                   mem, w_mem_kv[l].astype(BF16), w_out[l].astype(BF16),
                   ln1_g[l][None, :], ln1_b[l][None, :])
        x = _ffn(alpha, x.reshape(bsz * s, d), _pack_rows(w_ff1[l]), _pack_rows(w_ff2[l]),
                 ln2_g[l][None, :], ln2_b[l][None, :]).reshape(bsz, s, d)
    return x
```

```python
import functools

import jax
import jax.numpy as jnp
from jax import lax
from jax.experimental import pallas as pl
from jax.experimental.pallas import tpu as pltpu

F32 = jnp.float32
BF16 = jnp.bfloat16

GLA_HEADS = 4
GLA_DK = 64
GLA_DV = 128
GLA_KW = GLA_HEADS * GLA_DK
GLA_WIDTH = GLA_HEADS * GLA_DV
GLA_GATE_RANK = 16
GLA_TAU = 16.0
CONV_CH = 256
CONV_TAPS = 3
XA_HEADS = 4
XA_DH = 64
XA_WIDTH = XA_HEADS * XA_DH
LN_EPS = 1e-5
RMS_EPS = 1e-6
IN_SIZES = (GLA_KW, GLA_KW, GLA_WIDTH, GLA_WIDTH, GLA_GATE_RANK,
            CONV_CH, CONV_CH, CONV_CH, XA_WIDTH)

LANES = 128
SUBLANES = 8
VMEM_LIMIT_BYTES = 60 * 1024 * 1024

OFF_Q = 0
OFF_K = OFF_Q + GLA_KW
OFF_V = OFF_K + GLA_KW
OFF_G = OFF_V + GLA_WIDTH
OFF_CB = OFF_G + GLA_WIDTH
OFF_CC = OFF_CB + CONV_CH
OFF_CH = OFF_CC + CONV_CH
OFF_XQ = OFF_CH + CONV_CH
OFF_GATE = OFF_XQ + XA_WIDTH
IN_COLS_R = OFF_GATE + GLA_KW
GLA_COLS = OFF_CB

MIX_TM = 1024
GLA_C = 128
OUT_SUB = 4
FFN_TM = 2048
FFN_SUB = 8
SAFE_DECAY = 60.0

_NT = (((1,), (1,)), ((), ()))


def _layer_norm(r, g, b):
    mu = jnp.mean(r, axis=-1, keepdims=True)
    d = r - mu
    var = jnp.mean(d * d, axis=-1, keepdims=True)
    return d * lax.rsqrt(var + LN_EPS) * g + b


def _w_prep_kernel(wt_ref, wg_ref, wkv_ref, wout_ref, o_ref, wkvb_ref, woutb_ref):
    wkvb_ref[...] = wkv_ref[...].astype(BF16)
    woutb_ref[...] = wout_ref[...].astype(BF16)
    a0, rank = OFF_CB, GLA_GATE_RANK
    for j in range(0, OFF_GATE, LANES):
        src = j if j < a0 else j + rank
        o_ref[:, j:j + LANES] = wt_ref[src:src + LANES, :].T.astype(BF16)
    pad_rows = LANES - rank
    wa_t = jnp.concatenate([wt_ref[a0:a0 + rank, :], jnp.zeros((pad_rows, wt_ref.shape[1]), F32)],
                           axis=0)
    wg = jnp.concatenate([wg_ref[...], jnp.zeros((pad_rows, GLA_KW), F32)], axis=0)
    gate_t = jnp.dot(wg.T, wa_t, preferred_element_type=F32,
                     precision=lax.Precision.HIGHEST)
    o_ref[:, OFF_GATE:OFF_GATE + GLA_KW] = gate_t.T.astype(BF16)


def _head_masks():
    lane = lax.broadcasted_iota(jnp.int32, (1, GLA_KW), 1)
    return [(lane // GLA_DK) == h for h in range(GLA_HEADS)]


def _stack_heads(q, head_mask):
    return jnp.concatenate([jnp.where(m, q, 0.0) for m in head_mask], axis=0).astype(BF16)


def _gla_head_out(scores_h, o_inter_h, v_h, g_h, gn, causal):
    a_h = jnp.where(causal, scores_h, 0.0).astype(BF16)
    o_h = jnp.dot(a_h, v_h, preferred_element_type=F32) + o_inter_h
    ms = jnp.mean(o_h * o_h, axis=-1, keepdims=True)
    o_n = o_h * lax.rsqrt(ms + RMS_EPS) * gn
    return (o_n * (g_h * jax.nn.sigmoid(g_h))).astype(BF16)


def _mixer_kernel(alpha, x_ref, win_ref, bg_ref, cw_ref, gn_ref, mem_ref, wkv_ref, wout_ref,
                  g1_ref, b1_ref, wf1_ref, wf2_ref, o_ref, wf1b_ref, wf2b_ref, z_ref, b_ref,
                  sst_ref, y_ref, u_ref, s_ref, sc_ref, mkv_ref):
    wf1b_ref[...] = wf1_ref[...].astype(BF16)
    wf2b_ref[...] = wf2_ref[...].astype(BF16)

    tm = x_ref.shape[1]
    c = GLA_C
    n_chunks = tm // c
    chunk_rows = [slice(ci * c, (ci + 1) * c) for ci in range(n_chunks)]
    head_rows = [slice(h * c, (h + 1) * c) for h in range(GLA_HEADS)]
    head_keys = [slice(h * GLA_DK, (h + 1) * GLA_DK) for h in range(GLA_HEADS)]
    head_vals = [slice(h * GLA_DV, (h + 1) * GLA_DV) for h in range(GLA_HEADS)]

    @pl.when(pl.program_id(1) == 0)
    def _start_of_sequence():
        s_ref[...] = jnp.zeros_like(s_ref)
        u_ref[0:SUBLANES, :] = jnp.zeros((SUBLANES, CONV_CH), F32)
        mkv_ref[...] = jnp.dot(mem_ref[0].astype(BF16), wkv_ref[...],
                               preferred_element_type=F32).astype(BF16)

    xb = x_ref[0].astype(BF16)

    def proj(off, width):
        return jnp.dot(xb, win_ref[:, off:off + width], preferred_element_type=F32)

    def v_of(rows, h):
        return z_ref[rows, OFF_V + h * GLA_DV:OFF_V + (h + 1) * GLA_DV].astype(BF16)

    head_mask = _head_masks()
    row_i = lax.broadcasted_iota(jnp.int32, (c, c), 0)
    col_i = lax.broadcasted_iota(jnp.int32, (c, c), 1)
    causal = row_i >= col_i
    tri = causal.astype(BF16)
    q_scale = GLA_DK ** -0.5
    gn = gn_ref[...]

    pre = proj(OFF_GATE, GLA_KW) + bg_ref[...]
    la = (jnp.minimum(pre, 0.0) - jnp.log(1.0 + jnp.exp(-jnp.abs(pre)))) * (1.0 / GLA_TAU)
    z_ref[:, OFF_Q:OFF_Q + 2 * GLA_KW] = proj(OFF_Q, 2 * GLA_KW)
    for rows in chunk_rows:
        la_c = la[rows, :]
        hi = la_c.astype(BF16)
        lo = (la_c - hi.astype(F32)).astype(BF16)
        b_ref[rows, :] = (jnp.dot(tri, hi, preferred_element_type=F32)
                          + jnp.dot(tri, lo, preferred_element_type=F32))

    z_ref[:, OFF_V:OFF_V + 2 * GLA_WIDTH] = proj(OFF_V, 2 * GLA_WIDTH)
    xq = proj(OFF_XQ, XA_WIDTH) * (XA_DH ** -0.5)

    worst = jnp.zeros((1, GLA_KW), F32)
    qs_chunks, scores_chunks, dec_chunks, upd_chunks = [], [], [], []
    for rows in chunk_rows:
        b = b_ref[rows, :]
        b_last = b[c - 1:c, :]
        worst = jnp.maximum(worst, -b_last)
        kmat = z_ref[rows, OFF_K:OFF_K + GLA_KW]
        qs = _stack_heads(z_ref[rows, OFF_Q:OFF_Q + GLA_KW] * (jnp.exp(b) * q_scale),
                          head_mask)
        kt = (kmat * jnp.exp(-b)).astype(BF16)
        kd_t = (kmat * jnp.exp(b_last - b)).T.astype(BF16)
        qs_chunks.append(qs)
        scores_chunks.append(lax.dot_general(qs, kt, _NT, preferred_element_type=F32))
        dec_chunks.append(jnp.broadcast_to(jnp.exp(b_last), (GLA_DV, GLA_KW)).T)
        upd_chunks.append(jnp.concatenate(
            [jnp.dot(kd_t[head_keys[h], :], v_of(rows, h), preferred_element_type=F32)
             for h in range(GLA_HEADS)], axis=0))

    mk = mkv_ref[:, 0:XA_WIDTH]
    mv = mkv_ref[:, XA_WIDTH:2 * XA_WIDTH].astype(F32)
    probs = []
    for h in range(XA_HEADS):
        q_h = jnp.where(head_mask[h], xq, 0.0).astype(BF16)
        s = lax.dot_general(q_h, mk, _NT, preferred_element_type=F32)
        e = jnp.exp(s - jnp.max(s, axis=-1, keepdims=True))
        probs.append((e * (1.0 / jnp.sum(e, axis=-1, keepdims=True))).astype(BF16))


    s_c = s_ref[...]
    for ci, rows in enumerate(chunk_rows):
        sst_ref[ci] = s_c
        o_inter = jnp.dot(qs_chunks[ci], s_c.astype(BF16), preferred_element_type=F32)
        for h in range(GLA_HEADS):
            y_ref[rows, head_vals[h]] = _gla_head_out(
                scores_chunks[ci][head_rows[h], :], o_inter[head_rows[h], :],
                v_of(rows, h), z_ref[rows, OFF_G + h * GLA_DV:OFF_G + (h + 1) * GLA_DV],
                gn, causal)
        s_c = s_c * dec_chunks[ci] + upd_chunks[ci]
    s_ref[...] = s_c

    zc = proj(OFF_CB, 3 * CONV_CH)
    u = zc[:, CONV_CH:2 * CONV_CH] * zc[:, 2 * CONV_CH:]
    u_ref[SUBLANES:SUBLANES + tm, :] = u
    u_m1 = u_ref[SUBLANES - 1:SUBLANES - 1 + tm, :]
    u_m2 = u_ref[SUBLANES - 2:SUBLANES - 2 + tm, :]
    conv = cw_ref[0:1, :] * u_m2 + cw_ref[1:2, :] * u_m1 + cw_ref[2:3, :] * u
    y_conv = (zc[:, 0:CONV_CH] * conv).astype(BF16)
    y_ref[:, GLA_WIDTH:GLA_WIDTH + CONV_CH] = y_conv
    u_ref[0:SUBLANES, :] = u_ref[tm:tm + SUBLANES, :]

    xa = jnp.zeros((tm, XA_WIDTH), F32)
    for h in range(XA_HEADS):
        mv_h = jnp.where(head_mask[h], mv, 0.0).astype(BF16)
        xa = xa + jnp.dot(probs[h], mv_h, preferred_element_type=F32)
    y_xa = xa.astype(BF16)
    y_ref[:, GLA_WIDTH + CONV_CH:] = y_xa

    sub = tm // OUT_SUB
    for i in range(OUT_SUB):
        rows = slice(i * sub, (i + 1) * sub)
        mix = jnp.dot(y_ref[rows, :], wout_ref[...], preferred_element_type=F32)
        o_ref[0, rows, :] = _layer_norm(alpha * x_ref[0, rows, :] + mix,
                                        g1_ref[...], b1_ref[...])

    @pl.when(jnp.max(worst) > SAFE_DECAY)
    def _redo_with_rowwise_scores():
        sub8 = lax.broadcasted_iota(jnp.int32, (SUBLANES, GLA_KW), 0)
        lane8 = lax.broadcasted_iota(jnp.int32, (SUBLANES, GLA_KW), 1)
        head_sel8 = ((lane8 // GLA_DK) == sub8).astype(F32)

        def chunk_body(ci, carry):
            r0 = pl.multiple_of(ci * c, c)
            rows = pl.ds(r0, c)
            b = b_ref[rows, :]
            kmat = z_ref[rows, OFF_K:OFF_K + GLA_KW]

            def row_body(i, carry2):
                b_i = b_ref[pl.ds(r0 + i, 1), :]
                q_i = z_ref[pl.ds(r0 + i, 1), OFF_Q:OFF_Q + GLA_KW] * q_scale
                w = (kmat * jnp.exp(jnp.minimum(b_i - b, 0.0))).astype(BF16)
                q8 = (jnp.broadcast_to(q_i, (SUBLANES, GLA_KW)) * head_sel8).astype(BF16)
                r = lax.dot_general(q8, w, _NT, preferred_element_type=F32)
                for h in range(GLA_HEADS):
                    sc_ref[pl.ds(h * c + i, 1), :] = r[h:h + 1, :]
                return carry2

            lax.fori_loop(0, c, row_body, 0)
            qs = _stack_heads(z_ref[rows, OFF_Q:OFF_Q + GLA_KW] * (jnp.exp(b) * q_scale),
                              head_mask)
            o_inter = jnp.dot(qs, sst_ref[ci].astype(BF16), preferred_element_type=F32)
            for h in range(GLA_HEADS):
                y_ref[rows, head_vals[h]] = _gla_head_out(
                    sc_ref[head_rows[h], :], o_inter[head_rows[h], :], v_of(rows, h),
                    z_ref[rows, OFF_G + h * GLA_DV:OFF_G + (h + 1) * GLA_DV], gn, causal)
            return carry

        lax.fori_loop(0, n_chunks, chunk_body, 0)
        mix = jnp.dot(y_ref[...], wout_ref[...], preferred_element_type=F32)
        o_ref[0] = _layer_norm(alpha * x_ref[0] + mix, g1_ref[...], b1_ref[...])


def _ffn_kernel(alpha, x_ref, w1_ref, w2_ref, g2_ref, b2_ref, o_ref):
    sub = x_ref.shape[0] // FFN_SUB
    for i in range(FFN_SUB):
        rows = slice(i * sub, (i + 1) * sub)
        x = x_ref[rows, :]
        h = jnp.dot(x.astype(BF16), w1_ref[...], preferred_element_type=F32)
        h = jnp.square(jnp.maximum(h, 0.0)).astype(BF16)
        ff = jnp.dot(h, w2_ref[...], preferred_element_type=F32)
        o_ref[rows, :] = _layer_norm(alpha * x + ff, g2_ref[...], b2_ref[...])


def _const_spec(shape):
    return pl.BlockSpec(shape, lambda *_: (0,) * len(shape))


def _prep_weights(w_in_t, w_g, w_kv, w_out):
    assert w_in_t.shape[0] == sum(IN_SIZES) and sum(IN_SIZES[:4]) == OFF_CB
    return pl.pallas_call(
        _w_prep_kernel,
        out_shape=[jax.ShapeDtypeStruct((w_in_t.shape[1], IN_COLS_R), BF16),
                   jax.ShapeDtypeStruct(w_kv.shape, BF16),
                   jax.ShapeDtypeStruct(w_out.shape, BF16)],
        compiler_params=pltpu.CompilerParams(vmem_limit_bytes=VMEM_LIMIT_BYTES),
        name="w_prep",
    )(w_in_t, w_g, w_kv, w_out)


def _mixer(alpha, x, w_in_r, bg, cw, gn, mem, w_kv, w_out, g1, b1, w_ff1, w_ff2):
    bsz, s, d = x.shape
    m = mem.shape[1]
    tm = MIX_TM
    tps = s // tm
    n_steps = bsz * tps
    assert s % tm == 0 and tm % GLA_C == 0 and (tm // OUT_SUB) % GLA_C == 0
    f1_rows, f2_rows = w_ff1.shape[0] // n_steps, w_ff2.shape[0] // n_steps
    assert f1_rows * n_steps == w_ff1.shape[0] and f1_rows % (2 * SUBLANES) == 0
    assert f2_rows * n_steps == w_ff2.shape[0] and f2_rows % (2 * SUBLANES) == 0

    def step_rows(b, j):
        return (b * tps + j, 0)

    return pl.pallas_call(
        functools.partial(_mixer_kernel, alpha),
        grid=(bsz, tps),
        in_specs=[
            pl.BlockSpec((1, tm, d), lambda b, j: (b, j, 0)),
            _const_spec(w_in_r.shape), _const_spec(bg.shape),
            _const_spec(cw.shape), _const_spec(gn.shape),
            pl.BlockSpec((1, m, d), lambda b, j: (b, 0, 0)),
            _const_spec(w_kv.shape), _const_spec(w_out.shape),
            _const_spec(g1.shape), _const_spec(b1.shape),
            pl.BlockSpec((f1_rows, w_ff1.shape[1]), step_rows),
            pl.BlockSpec((f2_rows, w_ff2.shape[1]), step_rows),
        ],
        out_specs=[
            pl.BlockSpec((1, tm, d), lambda b, j: (b, j, 0)),
            pl.BlockSpec((f1_rows, w_ff1.shape[1]), step_rows),
            pl.BlockSpec((f2_rows, w_ff2.shape[1]), step_rows),
        ],
        out_shape=[
            jax.ShapeDtypeStruct((bsz, s, d), F32),
            jax.ShapeDtypeStruct(w_ff1.shape, BF16),
            jax.ShapeDtypeStruct(w_ff2.shape, BF16),
        ],
        scratch_shapes=[
            pltpu.VMEM((tm, GLA_COLS), F32),
            pltpu.VMEM((tm, GLA_KW), F32),
            pltpu.VMEM((tm // GLA_C, GLA_KW, GLA_DV), F32),
            pltpu.VMEM((tm, d), BF16),
            pltpu.VMEM((tm + SUBLANES, CONV_CH), F32),
            pltpu.VMEM((GLA_KW, GLA_DV), F32),
            pltpu.VMEM((GLA_HEADS * GLA_C, GLA_C), F32),
            pltpu.VMEM((m, 2 * XA_WIDTH), BF16),
        ],
        compiler_params=pltpu.CompilerParams(
            dimension_semantics=("arbitrary", "arbitrary"),
            vmem_limit_bytes=VMEM_LIMIT_BYTES),
        name="mixer",
    )(x, w_in_r, bg, cw, gn, mem, w_kv, w_out, g1, b1, w_ff1, w_ff2)


def _ffn(alpha, x2d, w1, w2, g2, b2):
    t, d = x2d.shape
    tm = FFN_TM
    assert t % tm == 0 and tm % FFN_SUB == 0
    return pl.pallas_call(
        functools.partial(_ffn_kernel, alpha),
        grid=(t // tm,),
        in_specs=[
            pl.BlockSpec((tm, d), lambda i: (i, 0)),
            _const_spec(w1.shape), _const_spec(w2.shape),
            _const_spec(g2.shape), _const_spec(b2.shape),
        ],
        out_specs=pl.BlockSpec((tm, d), lambda i: (i, 0)),
        out_shape=jax.ShapeDtypeStruct((t, d), F32),
        compiler_params=pltpu.CompilerParams(
            dimension_semantics=("arbitrary",), vmem_limit_bytes=VMEM_LIMIT_BYTES),
        name="ffn",
    )(x2d, w1, w2, g2, b2)


def kernel(x, mem, w_in, w_gate_up, b_gate, conv_w, gla_norm_g, w_mem_kv, w_out,
           ln1_g, ln1_b, w_ff1, w_ff2, ln2_g, ln2_b):
    depth = w_in.shape[0]
    bsz, s, d = x.shape
    alpha = (2.0 * depth) ** 0.25
    for l in range(depth):
        w_in_r, w_kv, w_o = _prep_weights(w_in[l].T, w_gate_up[l], w_mem_kv[l], w_out[l])
        x, w1, w2 = _mixer(alpha, x, w_in_r, b_gate[l][None, :], conv_w[l],
                           gla_norm_g[l][None, :], mem, w_kv, w_o,
                           ln1_g[l][None, :], ln1_b[l][None, :], w_ff1[l], w_ff2[l])
        x = _ffn(alpha, x.reshape(bsz * s, d), w1, w2,
                 ln2_g[l][None, :], ln2_b[l][None, :]).reshape(bsz, s, d)
    return x
```

```python
import functools

import jax
import jax.numpy as jnp
from jax import lax
from jax.experimental import pallas as pl
from jax.experimental.pallas import tpu as pltpu

F32 = jnp.float32
BF16 = jnp.bfloat16

GLA_HEADS = 4
GLA_DK = 64
GLA_DV = 128
GLA_KW = GLA_HEADS * GLA_DK
GLA_WIDTH = GLA_HEADS * GLA_DV
GLA_GATE_RANK = 16
GLA_TAU = 16.0
CONV_CH = 256
CONV_TAPS = 3
XA_HEADS = 4
XA_DH = 64
XA_WIDTH = XA_HEADS * XA_DH
LN_EPS = 1e-5
RMS_EPS = 1e-6
IN_SIZES = (GLA_KW, GLA_KW, GLA_WIDTH, GLA_WIDTH, GLA_GATE_RANK,
            CONV_CH, CONV_CH, CONV_CH, XA_WIDTH)

LANES = 128
SUBLANES = 8
VMEM_LIMIT_BYTES = 60 * 1024 * 1024

OFF_Q = 0
OFF_K = OFF_Q + GLA_KW
OFF_V = OFF_K + GLA_KW
OFF_G = OFF_V + GLA_WIDTH
OFF_CB = OFF_G + GLA_WIDTH
OFF_CC = OFF_CB + CONV_CH
OFF_CH = OFF_CC + CONV_CH
OFF_XQ = OFF_CH + CONV_CH
OFF_GATE = OFF_XQ + XA_WIDTH
IN_COLS_R = OFF_GATE + GLA_KW
GLA_COLS = OFF_CB

PREP_STEPS = 4
MIX_TM = 1024
GLA_C = 128
OUT_SUB = 4
FFN_TM = 1024
FFN_SUB = 4
SAFE_DECAY = 60.0

_NT = (((1,), (1,)), ((), ()))


def _layer_norm(r, g, b):
    mu = jnp.mean(r, axis=-1, keepdims=True)
    d = r - mu
    var = jnp.mean(d * d, axis=-1, keepdims=True)
    return d * lax.rsqrt(var + LN_EPS) * g + b


def _w_prep_kernel(wt_ref, wg_ref, wkv_ref, wout_ref, o_ref, wkvb_ref, woutb_ref):
    wkvb_ref[...] = wkv_ref[...].astype(BF16)
    woutb_ref[...] = wout_ref[...].astype(BF16)
    a0, rank = OFF_CB, GLA_GATE_RANK
    for j in range(0, OFF_GATE, LANES):
        src = j if j < a0 else j + rank
        o_ref[:, j:j + LANES] = wt_ref[src:src + LANES, :].T.astype(BF16)
    pad_rows = LANES - rank
    wa_t = jnp.concatenate([wt_ref[a0:a0 + rank, :], jnp.zeros((pad_rows, wt_ref.shape[1]), F32)],
                           axis=0)
    wg = jnp.concatenate([wg_ref[...], jnp.zeros((pad_rows, GLA_KW), F32)], axis=0)
    gate_t = jnp.dot(wg.T, wa_t, preferred_element_type=F32,
                     precision=lax.Precision.HIGHEST)
    o_ref[:, OFF_GATE:OFF_GATE + GLA_KW] = gate_t.T.astype(BF16)


def _head_masks():
    lane = lax.broadcasted_iota(jnp.int32, (1, GLA_KW), 1)
    return [(lane // GLA_DK) == h for h in range(GLA_HEADS)]


def _stack_heads(q, head_mask):
    return jnp.concatenate([jnp.where(m, q, 0.0) for m in head_mask], axis=0).astype(BF16)


def _gla_head_out(scores_h, o_inter_h, v_h, g_h, gn, causal):
    a_h = jnp.where(causal, scores_h, 0.0).astype(BF16)
    o_h = jnp.dot(a_h, v_h, preferred_element_type=F32) + o_inter_h
    ms = jnp.mean(o_h * o_h, axis=-1, keepdims=True)
    o_n = o_h * lax.rsqrt(ms + RMS_EPS) * gn
    return (o_n * (g_h * jax.nn.sigmoid(g_h))).astype(BF16)


def _mixer_kernel(alpha, x_ref, win_ref, bg_ref, cw_ref, gn_ref, mem_ref, wkv_ref, wout_ref,
                  g1_ref, b1_ref, wf1_ref, wf2_ref, o_ref, wf1b_ref, wf2b_ref, z_ref, b_ref,
                  sst_ref, y_ref, u_ref, s_ref, sc_ref, mkv_ref):
    wf1b_ref[...] = wf1_ref[...].astype(BF16)
    wf2b_ref[...] = wf2_ref[...].astype(BF16)

    tm = x_ref.shape[1]
    c = GLA_C
    n_chunks = tm // c
    chunk_rows = [slice(ci * c, (ci + 1) * c) for ci in range(n_chunks)]
    head_rows = [slice(h * c, (h + 1) * c) for h in range(GLA_HEADS)]
    head_keys = [slice(h * GLA_DK, (h + 1) * GLA_DK) for h in range(GLA_HEADS)]
    head_vals = [slice(h * GLA_DV, (h + 1) * GLA_DV) for h in range(GLA_HEADS)]

    @pl.when(pl.program_id(1) == 0)
    def _start_of_sequence():
        s_ref[...] = jnp.zeros_like(s_ref)
        u_ref[0:SUBLANES, :] = jnp.zeros((SUBLANES, CONV_CH), F32)
        mkv_ref[...] = jnp.dot(mem_ref[0].astype(BF16), wkv_ref[...],
                               preferred_element_type=F32).astype(BF16)

    xb = x_ref[0].astype(BF16)

    def proj(off, width):
        return jnp.dot(xb, win_ref[:, off:off + width], preferred_element_type=F32)

    def v_of(rows, h):
        return z_ref[rows, OFF_V + h * GLA_DV:OFF_V + (h + 1) * GLA_DV].astype(BF16)

    head_mask = _head_masks()
    row_i = lax.broadcasted_iota(jnp.int32, (c, c), 0)
    col_i = lax.broadcasted_iota(jnp.int32, (c, c), 1)
    causal = row_i >= col_i
    tri = causal.astype(BF16)
    q_scale = GLA_DK ** -0.5
    gn = gn_ref[...]

    pre = proj(OFF_GATE, GLA_KW) + bg_ref[...]
    la = (jnp.minimum(pre, 0.0) - jnp.log(1.0 + jnp.exp(-jnp.abs(pre)))) * (1.0 / GLA_TAU)
    z_ref[:, OFF_Q:OFF_Q + 2 * GLA_KW] = proj(OFF_Q, 2 * GLA_KW)
    for rows in chunk_rows:
        la_c = la[rows, :]
        hi = la_c.astype(BF16)
        lo = (la_c - hi.astype(F32)).astype(BF16)
        b_ref[rows, :] = (jnp.dot(tri, hi, preferred_element_type=F32)
                          + jnp.dot(tri, lo, preferred_element_type=F32))

    z_ref[:, OFF_V:OFF_V + 2 * GLA_WIDTH] = proj(OFF_V, 2 * GLA_WIDTH)
    xq = proj(OFF_XQ, XA_WIDTH) * (XA_DH ** -0.5)

    worst = jnp.zeros((1, GLA_KW), F32)
    qs_chunks, scores_chunks, dec_chunks, upd_chunks = [], [], [], []
    for rows in chunk_rows:
        b = b_ref[rows, :]
        b_last = b[c - 1:c, :]
        worst = jnp.maximum(worst, -b_last)
        kmat = z_ref[rows, OFF_K:OFF_K + GLA_KW]
        qs = _stack_heads(z_ref[rows, OFF_Q:OFF_Q + GLA_KW] * (jnp.exp(b) * q_scale),
                          head_mask)
        kt = (kmat * jnp.exp(-b)).astype(BF16)
        kd_t = (kmat * jnp.exp(b_last - b)).T.astype(BF16)
        qs_chunks.append(qs)
        scores_chunks.append(lax.dot_general(qs, kt, _NT, preferred_element_type=F32))
        dec_chunks.append(jnp.broadcast_to(jnp.exp(b_last), (GLA_DV, GLA_KW)).T)
        upd_chunks.append(jnp.concatenate(
            [jnp.dot(kd_t[head_keys[h], :], v_of(rows, h), preferred_element_type=F32)
             for h in range(GLA_HEADS)], axis=0))

    mk = mkv_ref[:, 0:XA_WIDTH]
    mv = mkv_ref[:, XA_WIDTH:2 * XA_WIDTH].astype(F32)
    probs = []
    for h in range(XA_HEADS):
        q_h = jnp.where(head_mask[h], xq, 0.0).astype(BF16)
        s = lax.dot_general(q_h, mk, _NT, preferred_element_type=F32)
        e = jnp.exp(s - jnp.max(s, axis=-1, keepdims=True))
        probs.append((e * (1.0 / jnp.sum(e, axis=-1, keepdims=True))).astype(BF16))


    s_c = s_ref[...]
    for ci, rows in enumerate(chunk_rows):
        sst_ref[ci] = s_c
        o_inter = jnp.dot(qs_chunks[ci], s_c.astype(BF16), preferred_element_type=F32)
        for h in range(GLA_HEADS):
            y_ref[rows, head_vals[h]] = _gla_head_out(
                scores_chunks[ci][head_rows[h], :], o_inter[head_rows[h], :],
                v_of(rows, h), z_ref[rows, OFF_G + h * GLA_DV:OFF_G + (h + 1) * GLA_DV],
                gn, causal)
        s_c = s_c * dec_chunks[ci] + upd_chunks[ci]
    s_ref[...] = s_c

    zc = proj(OFF_CB, 3 * CONV_CH)
    u = zc[:, CONV_CH:2 * CONV_CH] * zc[:, 2 * CONV_CH:]
    u_ref[SUBLANES:SUBLANES + tm, :] = u
    u_m1 = u_ref[SUBLANES - 1:SUBLANES - 1 + tm, :]
    u_m2 = u_ref[SUBLANES - 2:SUBLANES - 2 + tm, :]
    conv = cw_ref[0:1, :] * u_m2 + cw_ref[1:2, :] * u_m1 + cw_ref[2:3, :] * u
    y_conv = (zc[:, 0:CONV_CH] * conv).astype(BF16)
    y_ref[:, GLA_WIDTH:GLA_WIDTH + CONV_CH] = y_conv
    u_ref[0:SUBLANES, :] = u_ref[tm:tm + SUBLANES, :]

    xa = jnp.zeros((tm, XA_WIDTH), F32)
    for h in range(XA_HEADS):
        mv_h = jnp.where(head_mask[h], mv, 0.0).astype(BF16)
        xa = xa + jnp.dot(probs[h], mv_h, preferred_element_type=F32)
    y_xa = xa.astype(BF16)
    y_ref[:, GLA_WIDTH + CONV_CH:] = y_xa

    sub = tm // OUT_SUB
    for i in range(OUT_SUB):
        rows = slice(i * sub, (i + 1) * sub)
        mix = jnp.dot(y_ref[rows, :], wout_ref[...], preferred_element_type=F32)
        o_ref[0, rows, :] = _layer_norm(alpha * x_ref[0, rows, :] + mix,
                                        g1_ref[...], b1_ref[...])

    @pl.when(jnp.max(worst) > SAFE_DECAY)
    def _redo_with_rowwise_scores():
        sub8 = lax.broadcasted_iota(jnp.int32, (SUBLANES, GLA_KW), 0)
        lane8 = lax.broadcasted_iota(jnp.int32, (SUBLANES, GLA_KW), 1)
        head_sel8 = ((lane8 // GLA_DK) == sub8).astype(F32)

        def chunk_body(ci, carry):
            r0 = pl.multiple_of(ci * c, c)
            rows = pl.ds(r0, c)
            b = b_ref[rows, :]
            kmat = z_ref[rows, OFF_K:OFF_K + GLA_KW]

            def row_body(i, carry2):
                b_i = b_ref[pl.ds(r0 + i, 1), :]
                q_i = z_ref[pl.ds(r0 + i, 1), OFF_Q:OFF_Q + GLA_KW] * q_scale
                w = (kmat * jnp.exp(jnp.minimum(b_i - b, 0.0))).astype(BF16)
                q8 = (jnp.broadcast_to(q_i, (SUBLANES, GLA_KW)) * head_sel8).astype(BF16)
                r = lax.dot_general(q8, w, _NT, preferred_element_type=F32)
                for h in range(GLA_HEADS):
                    sc_ref[pl.ds(h * c + i, 1), :] = r[h:h + 1, :]
                return carry2

            lax.fori_loop(0, c, row_body, 0)
            qs = _stack_heads(z_ref[rows, OFF_Q:OFF_Q + GLA_KW] * (jnp.exp(b) * q_scale),
                              head_mask)
            o_inter = jnp.dot(qs, sst_ref[ci].astype(BF16), preferred_element_type=F32)
            for h in range(GLA_HEADS):
                y_ref[rows, head_vals[h]] = _gla_head_out(
                    sc_ref[head_rows[h], :], o_inter[head_rows[h], :], v_of(rows, h),
                    z_ref[rows, OFF_G + h * GLA_DV:OFF_G + (h + 1) * GLA_DV], gn, causal)
            return carry

        lax.fori_loop(0, n_chunks, chunk_body, 0)
        mix = jnp.dot(y_ref[...], wout_ref[...], preferred_element_type=F32)
        o_ref[0] = _layer_norm(alpha * x_ref[0] + mix, g1_ref[...], b1_ref[...])


def _ffn_kernel(alpha, x_ref, w1_ref, w2_ref, g2_ref, b2_ref, o_ref):
    sub = x_ref.shape[0] // FFN_SUB
    for i in range(FFN_SUB):
        rows = slice(i * sub, (i + 1) * sub)
        x = x_ref[rows, :]
        h = jnp.dot(x.astype(BF16), w1_ref[...], preferred_element_type=F32)
        h = jnp.square(jnp.maximum(h, 0.0)).astype(BF16)
        ff = jnp.dot(h, w2_ref[...], preferred_element_type=F32)
        o_ref[rows, :] = _layer_norm(alpha * x + ff, g2_ref[...], b2_ref[...])


def _const_spec(shape):
    return pl.BlockSpec(shape, lambda *_: (0,) * len(shape))


def _prep_weights(w_in_t, w_g, w_kv, w_out):
    n_in, d = w_in_t.shape
    assert n_in == sum(IN_SIZES) and sum(IN_SIZES[:4]) == OFF_CB
    assert w_kv.shape[0] == d and w_out.shape[0] == d and d % (PREP_STEPS * LANES) == 0
    kb = d // PREP_STEPS
    return pl.pallas_call(
        _w_prep_kernel,
        grid=(PREP_STEPS,),
        in_specs=[
            pl.BlockSpec((n_in, kb), lambda i: (0, i)),
            _const_spec(w_g.shape),
            pl.BlockSpec((kb, w_kv.shape[1]), lambda i: (i, 0)),
            pl.BlockSpec((kb, w_out.shape[1]), lambda i: (i, 0)),
        ],
        out_specs=[
            pl.BlockSpec((kb, IN_COLS_R), lambda i: (i, 0)),
            pl.BlockSpec((kb, w_kv.shape[1]), lambda i: (i, 0)),
            pl.BlockSpec((kb, w_out.shape[1]), lambda i: (i, 0)),
        ],
        out_shape=[jax.ShapeDtypeStruct((d, IN_COLS_R), BF16),
                   jax.ShapeDtypeStruct(w_kv.shape, BF16),
                   jax.ShapeDtypeStruct(w_out.shape, BF16)],
        compiler_params=pltpu.CompilerParams(dimension_semantics=("arbitrary",),
                                             vmem_limit_bytes=VMEM_LIMIT_BYTES),
        name="w_prep",
    )(w_in_t, w_g, w_kv, w_out)


def _mixer(alpha, x, w_in_r, bg, cw, gn, mem, w_kv, w_out, g1, b1, w_ff1, w_ff2):
    bsz, s, d = x.shape
    m = mem.shape[1]
    tm = MIX_TM
    tps = s // tm
    n_steps = bsz * tps
    assert s % tm == 0 and tm % GLA_C == 0 and (tm // OUT_SUB) % GLA_C == 0
    f1_rows, f2_rows = w_ff1.shape[0] // n_steps, w_ff2.shape[0] // n_steps
    assert f1_rows * n_steps == w_ff1.shape[0] and f1_rows % (2 * SUBLANES) == 0
    assert f2_rows * n_steps == w_ff2.shape[0] and f2_rows % (2 * SUBLANES) == 0

    def step_rows(b, j):
        return (b * tps + j, 0)

    return pl.pallas_call(
        functools.partial(_mixer_kernel, alpha),
        grid=(bsz, tps),
        in_specs=[
            pl.BlockSpec((1, tm, d), lambda b, j: (b, j, 0)),
            _const_spec(w_in_r.shape), _const_spec(bg.shape),
            _const_spec(cw.shape), _const_spec(gn.shape),
            pl.BlockSpec((1, m, d), lambda b, j: (b, 0, 0)),
            _const_spec(w_kv.shape), _const_spec(w_out.shape),
            _const_spec(g1.shape), _const_spec(b1.shape),
            pl.BlockSpec((f1_rows, w_ff1.shape[1]), step_rows),
            pl.BlockSpec((f2_rows, w_ff2.shape[1]), step_rows),
        ],
        out_specs=[
            pl.BlockSpec((1, tm, d), lambda b, j: (b, j, 0)),
            pl.BlockSpec((f1_rows, w_ff1.shape[1]), step_rows),
            pl.BlockSpec((f2_rows, w_ff2.shape[1]), step_rows),
        ],
        out_shape=[
            jax.ShapeDtypeStruct((bsz, s, d), F32),
            jax.ShapeDtypeStruct(w_ff1.shape, BF16),
            jax.ShapeDtypeStruct(w_ff2.shape, BF16),
        ],
        scratch_shapes=[
            pltpu.VMEM((tm, GLA_COLS), F32),
            pltpu.VMEM((tm, GLA_KW), F32),
            pltpu.VMEM((tm // GLA_C, GLA_KW, GLA_DV), F32),
            pltpu.VMEM((tm, d), BF16),
            pltpu.VMEM((tm + SUBLANES, CONV_CH), F32),
            pltpu.VMEM((GLA_KW, GLA_DV), F32),
            pltpu.VMEM((GLA_HEADS * GLA_C, GLA_C), F32),
            pltpu.VMEM((m, 2 * XA_WIDTH), BF16),
        ],
        compiler_params=pltpu.CompilerParams(
            dimension_semantics=("arbitrary", "arbitrary"),
            vmem_limit_bytes=VMEM_LIMIT_BYTES),
        name="mixer",
    )(x, w_in_r, bg, cw, gn, mem, w_kv, w_out, g1, b1, w_ff1, w_ff2)


def _ffn(alpha, x2d, w1, w2, g2, b2):
    t, d = x2d.shape
    tm = FFN_TM
    assert t % tm == 0 and tm % FFN_SUB == 0
    return pl.pallas_call(
        functools.partial(_ffn_kernel, alpha),
        grid=(t // tm,),
        in_specs=[
            pl.BlockSpec((tm, d), lambda i: (i, 0)),
            _const_spec(w1.shape), _const_spec(w2.shape),
            _const_spec(g2.shape), _const_spec(b2.shape),
        ],
        out_specs=pl.BlockSpec((tm, d), lambda i: (i, 0)),
        out_shape=jax.ShapeDtypeStruct((t, d), F32),
        compiler_params=pltpu.CompilerParams(
            dimension_semantics=("arbitrary",), vmem_limit_bytes=VMEM_LIMIT_BYTES),
        name="ffn",
    )(x2d, w1, w2, g2, b2)


def kernel(x, mem, w_in, w_gate_up, b_gate, conv_w, gla_norm_g, w_mem_kv, w_out,
           ln1_g, ln1_b, w_ff1, w_ff2, ln2_g, ln2_b):
    depth = w_in.shape[0]
    bsz, s, d = x.shape
    alpha = (2.0 * depth) ** 0.25
    for l in range(depth):
        w_in_r, w_kv, w_o = _prep_weights(w_in[l].T, w_gate_up[l], w_mem_kv[l], w_out[l])
        x, w1, w2 = _mixer(alpha, x, w_in_r, b_gate[l][None, :], conv_w[l],
                           gla_norm_g[l][None, :], mem, w_kv, w_o,
                           ln1_g[l][None, :], ln1_b[l][None, :], w_ff1[l], w_ff2[l])
        x = _ffn(alpha, x.reshape(bsz * s, d), w1, w2,
                 ln2_g[l][None, :], ln2_b[l][None, :]).reshape(bsz, s, d)
    return x
```

```python
import functools

import jax
import jax.numpy as jnp
from jax import lax
from jax.experimental import pallas as pl
from jax.experimental.pallas import tpu as pltpu

F32 = jnp.float32
BF16 = jnp.bfloat16

GLA_HEADS = 4
GLA_DK = 64
GLA_DV = 128
GLA_KW = GLA_HEADS * GLA_DK
GLA_WIDTH = GLA_HEADS * GLA_DV
GLA_GATE_RANK = 16
GLA_TAU = 16.0
CONV_CH = 256
CONV_TAPS = 3
XA_HEADS = 4
XA_DH = 64
XA_WIDTH = XA_HEADS * XA_DH
LN_EPS = 1e-5
RMS_EPS = 1e-6
IN_SIZES = (GLA_KW, GLA_KW, GLA_WIDTH, GLA_WIDTH, GLA_GATE_RANK,
            CONV_CH, CONV_CH, CONV_CH, XA_WIDTH)

LANES = 128
SUBLANES = 8
VMEM_LIMIT_BYTES = 60 * 1024 * 1024

OFF_Q = 0
OFF_K = OFF_Q + GLA_KW
OFF_V = OFF_K + GLA_KW
OFF_G = OFF_V + GLA_WIDTH
OFF_CB = OFF_G + GLA_WIDTH
OFF_CC = OFF_CB + CONV_CH
OFF_CH = OFF_CC + CONV_CH
OFF_XQ = OFF_CH + CONV_CH
OFF_GATE = OFF_XQ + XA_WIDTH
IN_COLS_R = OFF_GATE + GLA_KW
GLA_COLS = OFF_CB

MIX_TM = 1024
GLA_C = 128
OUT_SUB = 4
FFN_TM = 1024
FFN_SUB = 2
SAFE_DECAY = 60.0

_NT = (((1,), (1,)), ((), ()))


def _layer_norm(r, g, b):
    mu = jnp.mean(r, axis=-1, keepdims=True)
    d = r - mu
    var = jnp.mean(d * d, axis=-1, keepdims=True)
    return d * lax.rsqrt(var + LN_EPS) * g + b


def _w_prep_kernel(wt_ref, wg_ref, wkv_ref, wout_ref, o_ref, wkvb_ref, woutb_ref):
    wkvb_ref[...] = wkv_ref[...].astype(BF16)
    woutb_ref[...] = wout_ref[...].astype(BF16)
    a0, rank = OFF_CB, GLA_GATE_RANK
    for j in range(0, OFF_GATE, LANES):
        src = j if j < a0 else j + rank
        o_ref[:, j:j + LANES] = wt_ref[src:src + LANES, :].T.astype(BF16)
    pad_rows = LANES - rank
    wa_t = jnp.concatenate([wt_ref[a0:a0 + rank, :], jnp.zeros((pad_rows, wt_ref.shape[1]), F32)],
                           axis=0)
    wg = jnp.concatenate([wg_ref[...], jnp.zeros((pad_rows, GLA_KW), F32)], axis=0)
    gate_t = jnp.dot(wg.T, wa_t, preferred_element_type=F32,
                     precision=lax.Precision.HIGHEST)
    o_ref[:, OFF_GATE:OFF_GATE + GLA_KW] = gate_t.T.astype(BF16)


def _head_masks():
    lane = lax.broadcasted_iota(jnp.int32, (1, GLA_KW), 1)
    return [(lane // GLA_DK) == h for h in range(GLA_HEADS)]


def _stack_heads(q, head_mask):
    return jnp.concatenate([jnp.where(m, q, 0.0) for m in head_mask], axis=0).astype(BF16)


def _gla_head_out(scores_h, o_inter_h, v_h, g_h, gn, causal):
    a_h = jnp.where(causal, scores_h, 0.0).astype(BF16)
    o_h = jnp.dot(a_h, v_h, preferred_element_type=F32) + o_inter_h
    ms = jnp.mean(o_h * o_h, axis=-1, keepdims=True)
    o_n = o_h * lax.rsqrt(ms + RMS_EPS) * gn
    return (o_n * (g_h * jax.nn.sigmoid(g_h))).astype(BF16)


def _mixer_kernel(alpha, x_ref, win_ref, bg_ref, cw_ref, gn_ref, mem_ref, wkv_ref, wout_ref,
                  g1_ref, b1_ref, wf1_ref, wf2_ref, o_ref, wf1b_ref, wf2b_ref, z_ref, b_ref,
                  sst_ref, y_ref, u_ref, s_ref, sc_ref, mkv_ref):
    wf1b_ref[...] = wf1_ref[...].astype(BF16)
    wf2b_ref[...] = wf2_ref[...].astype(BF16)

    tm = x_ref.shape[1]
    c = GLA_C
    n_chunks = tm // c
    chunk_rows = [slice(ci * c, (ci + 1) * c) for ci in range(n_chunks)]
    head_rows = [slice(h * c, (h + 1) * c) for h in range(GLA_HEADS)]
    head_keys = [slice(h * GLA_DK, (h + 1) * GLA_DK) for h in range(GLA_HEADS)]
    head_vals = [slice(h * GLA_DV, (h + 1) * GLA_DV) for h in range(GLA_HEADS)]

    @pl.when(pl.program_id(1) == 0)
    def _start_of_sequence():
        s_ref[...] = jnp.zeros_like(s_ref)
        u_ref[0:SUBLANES, :] = jnp.zeros((SUBLANES, CONV_CH), F32)
        mkv_ref[...] = jnp.dot(mem_ref[0].astype(BF16), wkv_ref[...],
                               preferred_element_type=F32).astype(BF16)

    xb = x_ref[0].astype(BF16)

    def proj(off, width):
        return jnp.dot(xb, win_ref[:, off:off + width], preferred_element_type=F32)

    def v_of(rows, h):
        return z_ref[rows, OFF_V + h * GLA_DV:OFF_V + (h + 1) * GLA_DV].astype(BF16)

    head_mask = _head_masks()
    row_i = lax.broadcasted_iota(jnp.int32, (c, c), 0)
    col_i = lax.broadcasted_iota(jnp.int32, (c, c), 1)
    causal = row_i >= col_i
    tri = causal.astype(BF16)
    q_scale = GLA_DK ** -0.5
    gn = gn_ref[...]

    pre = proj(OFF_GATE, GLA_KW) + bg_ref[...]
    la = (jnp.minimum(pre, 0.0) - jnp.log(1.0 + jnp.exp(-jnp.abs(pre)))) * (1.0 / GLA_TAU)
    z_ref[:, OFF_Q:OFF_Q + 2 * GLA_KW] = proj(OFF_Q, 2 * GLA_KW)
    for rows in chunk_rows:
        la_c = la[rows, :]
        hi = la_c.astype(BF16)
        lo = (la_c - hi.astype(F32)).astype(BF16)
        b_ref[rows, :] = (jnp.dot(tri, hi, preferred_element_type=F32)
                          + jnp.dot(tri, lo, preferred_element_type=F32))

    z_ref[:, OFF_V:OFF_V + 2 * GLA_WIDTH] = proj(OFF_V, 2 * GLA_WIDTH)
    xq = proj(OFF_XQ, XA_WIDTH) * (XA_DH ** -0.5)

    worst = jnp.zeros((1, GLA_KW), F32)
    qs_chunks, scores_chunks, dec_chunks, upd_chunks = [], [], [], []
    for rows in chunk_rows:
        b = b_ref[rows, :]
        b_last = b[c - 1:c, :]
        worst = jnp.maximum(worst, -b_last)
        kmat = z_ref[rows, OFF_K:OFF_K + GLA_KW]
        qs = _stack_heads(z_ref[rows, OFF_Q:OFF_Q + GLA_KW] * (jnp.exp(b) * q_scale),
                          head_mask)
        kt = (kmat * jnp.exp(-b)).astype(BF16)
        kd_t = (kmat * jnp.exp(b_last - b)).T.astype(BF16)
        qs_chunks.append(qs)
        scores_chunks.append(lax.dot_general(qs, kt, _NT, preferred_element_type=F32))
        dec_chunks.append(jnp.broadcast_to(jnp.exp(b_last), (GLA_DV, GLA_KW)).T)
        upd_chunks.append(jnp.concatenate(
            [jnp.dot(kd_t[head_keys[h], :], v_of(rows, h), preferred_element_type=F32)
             for h in range(GLA_HEADS)], axis=0))

    mk = mkv_ref[:, 0:XA_WIDTH]
    mv = mkv_ref[:, XA_WIDTH:2 * XA_WIDTH].astype(F32)
    probs = []
    for h in range(XA_HEADS):
        q_h = jnp.where(head_mask[h], xq, 0.0).astype(BF16)
        s = lax.dot_general(q_h, mk, _NT, preferred_element_type=F32)
        e = jnp.exp(s - jnp.max(s, axis=-1, keepdims=True))
        probs.append((e * (1.0 / jnp.sum(e, axis=-1, keepdims=True))).astype(BF16))


    s_c = s_ref[...]
    for ci, rows in enumerate(chunk_rows):
        sst_ref[ci] = s_c
        o_inter = jnp.dot(qs_chunks[ci], s_c.astype(BF16), preferred_element_type=F32)
        for h in range(GLA_HEADS):
            y_ref[rows, head_vals[h]] = _gla_head_out(
                scores_chunks[ci][head_rows[h], :], o_inter[head_rows[h], :],
                v_of(rows, h), z_ref[rows, OFF_G + h * GLA_DV:OFF_G + (h + 1) * GLA_DV],
                gn, causal)
        s_c = s_c * dec_chunks[ci] + upd_chunks[ci]
    s_ref[...] = s_c

    zc = proj(OFF_CB, 3 * CONV_CH)
    u = zc[:, CONV_CH:2 * CONV_CH] * zc[:, 2 * CONV_CH:]
    u_ref[SUBLANES:SUBLANES + tm, :] = u
    u_m1 = u_ref[SUBLANES - 1:SUBLANES - 1 + tm, :]
    u_m2 = u_ref[SUBLANES - 2:SUBLANES - 2 + tm, :]
    conv = cw_ref[0:1, :] * u_m2 + cw_ref[1:2, :] * u_m1 + cw_ref[2:3, :] * u
    y_conv = (zc[:, 0:CONV_CH] * conv).astype(BF16)
    y_ref[:, GLA_WIDTH:GLA_WIDTH + CONV_CH] = y_conv
    u_ref[0:SUBLANES, :] = u_ref[tm:tm + SUBLANES, :]

    xa = jnp.zeros((tm, XA_WIDTH), F32)
    for h in range(XA_HEADS):
        mv_h = jnp.where(head_mask[h], mv, 0.0).astype(BF16)
        xa = xa + jnp.dot(probs[h], mv_h, preferred_element_type=F32)
    y_xa = xa.astype(BF16)
    y_ref[:, GLA_WIDTH + CONV_CH:] = y_xa

    sub = tm // OUT_SUB
    for i in range(OUT_SUB):
        rows = slice(i * sub, (i + 1) * sub)
        mix = jnp.dot(y_ref[rows, :], wout_ref[...], preferred_element_type=F32)
        o_ref[0, rows, :] = _layer_norm(alpha * x_ref[0, rows, :] + mix,
                                        g1_ref[...], b1_ref[...])

    @pl.when(jnp.max(worst) > SAFE_DECAY)
    def _redo_with_rowwise_scores():
        sub8 = lax.broadcasted_iota(jnp.int32, (SUBLANES, GLA_KW), 0)
        lane8 = lax.broadcasted_iota(jnp.int32, (SUBLANES, GLA_KW), 1)
        head_sel8 = ((lane8 // GLA_DK) == sub8).astype(F32)

        def chunk_body(ci, carry):
            r0 = pl.multiple_of(ci * c, c)
            rows = pl.ds(r0, c)
            b = b_ref[rows, :]
            kmat = z_ref[rows, OFF_K:OFF_K + GLA_KW]

            def row_body(i, carry2):
                b_i = b_ref[pl.ds(r0 + i, 1), :]
                q_i = z_ref[pl.ds(r0 + i, 1), OFF_Q:OFF_Q + GLA_KW] * q_scale
                w = (kmat * jnp.exp(jnp.minimum(b_i - b, 0.0))).astype(BF16)
                q8 = (jnp.broadcast_to(q_i, (SUBLANES, GLA_KW)) * head_sel8).astype(BF16)
                r = lax.dot_general(q8, w, _NT, preferred_element_type=F32)
                for h in range(GLA_HEADS):
                    sc_ref[pl.ds(h * c + i, 1), :] = r[h:h + 1, :]
                return carry2

            lax.fori_loop(0, c, row_body, 0)
            qs = _stack_heads(z_ref[rows, OFF_Q:OFF_Q + GLA_KW] * (jnp.exp(b) * q_scale),
                              head_mask)
            o_inter = jnp.dot(qs, sst_ref[ci].astype(BF16), preferred_element_type=F32)
            for h in range(GLA_HEADS):
                y_ref[rows, head_vals[h]] = _gla_head_out(
                    sc_ref[head_rows[h], :], o_inter[head_rows[h], :], v_of(rows, h),
                    z_ref[rows, OFF_G + h * GLA_DV:OFF_G + (h + 1) * GLA_DV], gn, causal)
            return carry

        lax.fori_loop(0, n_chunks, chunk_body, 0)
        mix = jnp.dot(y_ref[...], wout_ref[...], preferred_element_type=F32)
        o_ref[0] = _layer_norm(alpha * x_ref[0] + mix, g1_ref[...], b1_ref[...])


def _ffn_kernel(alpha, x_ref, w1_ref, w2_ref, g2_ref, b2_ref, o_ref):
    sub = x_ref.shape[0] // FFN_SUB
    for i in range(FFN_SUB):
        rows = slice(i * sub, (i + 1) * sub)
        x = x_ref[rows, :]
        h = jnp.dot(x.astype(BF16), w1_ref[...], preferred_element_type=F32)
        h = jnp.square(jnp.maximum(h, 0.0)).astype(BF16)
        ff = jnp.dot(h, w2_ref[...], preferred_element_type=F32)
        o_ref[rows, :] = _layer_norm(alpha * x + ff, g2_ref[...], b2_ref[...])


def _const_spec(shape):
    return pl.BlockSpec(shape, lambda *_: (0,) * len(shape))


def _prep_weights(w_in_t, w_g, w_kv, w_out):
    assert w_in_t.shape[0] == sum(IN_SIZES) and sum(IN_SIZES[:4]) == OFF_CB
    return pl.pallas_call(
        _w_prep_kernel,
        out_shape=[jax.ShapeDtypeStruct((w_in_t.shape[1], IN_COLS_R), BF16),
                   jax.ShapeDtypeStruct(w_kv.shape, BF16),
                   jax.ShapeDtypeStruct(w_out.shape, BF16)],
        compiler_params=pltpu.CompilerParams(vmem_limit_bytes=VMEM_LIMIT_BYTES),
        name="w_prep",
    )(w_in_t, w_g, w_kv, w_out)


def _mixer(alpha, x, w_in_r, bg, cw, gn, mem, w_kv, w_out, g1, b1, w_ff1, w_ff2):
    bsz, s, d = x.shape
    m = mem.shape[1]
    tm = MIX_TM
    tps = s // tm
    n_steps = bsz * tps
    assert s % tm == 0 and tm % GLA_C == 0 and (tm // OUT_SUB) % GLA_C == 0
    f1_rows, f2_rows = w_ff1.shape[0] // n_steps, w_ff2.shape[0] // n_steps
    assert f1_rows * n_steps == w_ff1.shape[0] and f1_rows % (2 * SUBLANES) == 0
    assert f2_rows * n_steps == w_ff2.shape[0] and f2_rows % (2 * SUBLANES) == 0

    def step_rows(b, j):
        return (b * tps + j, 0)

    return pl.pallas_call(
        functools.partial(_mixer_kernel, alpha),
        grid=(bsz, tps),
        in_specs=[
            pl.BlockSpec((1, tm, d), lambda b, j: (b, j, 0)),
            _const_spec(w_in_r.shape), _const_spec(bg.shape),
            _const_spec(cw.shape), _const_spec(gn.shape),
            pl.BlockSpec((1, m, d), lambda b, j: (b, 0, 0)),
            _const_spec(w_kv.shape), _const_spec(w_out.shape),
            _const_spec(g1.shape), _const_spec(b1.shape),
            pl.BlockSpec((f1_rows, w_ff1.shape[1]), step_rows),
            pl.BlockSpec((f2_rows, w_ff2.shape[1]), step_rows),
        ],
        out_specs=[
            pl.BlockSpec((1, tm, d), lambda b, j: (b, j, 0)),
            pl.BlockSpec((f1_rows, w_ff1.shape[1]), step_rows),
            pl.BlockSpec((f2_rows, w_ff2.shape[1]), step_rows),
        ],
        out_shape=[
            jax.ShapeDtypeStruct((bsz, s, d), F32),
            jax.ShapeDtypeStruct(w_ff1.shape, BF16),
            jax.ShapeDtypeStruct(w_ff2.shape, BF16),
        ],
        scratch_shapes=[
            pltpu.VMEM((tm, GLA_COLS), F32),
            pltpu.VMEM((tm, GLA_KW), F32),
            pltpu.VMEM((tm // GLA_C, GLA_KW, GLA_DV), F32),
            pltpu.VMEM((tm, d), BF16),
            pltpu.VMEM((tm + SUBLANES, CONV_CH), F32),
            pltpu.VMEM((GLA_KW, GLA_DV), F32),
            pltpu.VMEM((GLA_HEADS * GLA_C, GLA_C), F32),
            pltpu.VMEM((m, 2 * XA_WIDTH), BF16),
        ],
        compiler_params=pltpu.CompilerParams(
            dimension_semantics=("arbitrary", "arbitrary"),
            vmem_limit_bytes=VMEM_LIMIT_BYTES),
        name="mixer",
    )(x, w_in_r, bg, cw, gn, mem, w_kv, w_out, g1, b1, w_ff1, w_ff2)


def _ffn(alpha, x2d, w1, w2, g2, b2):
    t, d = x2d.shape
    tm = FFN_TM
    assert t % tm == 0 and tm % FFN_SUB == 0
    return pl.pallas_call(
        functools.partial(_ffn_kernel, alpha),
        grid=(t // tm,),
        in_specs=[
            pl.BlockSpec((tm, d), lambda i: (i, 0)),
            _const_spec(w1.shape), _const_spec(w2.shape),
            _const_spec(g2.shape), _const_spec(b2.shape),
        ],
        out_specs=pl.BlockSpec((tm, d), lambda i: (i, 0)),
        out_shape=jax.ShapeDtypeStruct((t, d), F32),
        compiler_params=pltpu.CompilerParams(
            dimension_semantics=("arbitrary",), vmem_limit_bytes=VMEM_LIMIT_BYTES),
        name="ffn",
    )(x2d, w1, w2, g2, b2)


def kernel(x, mem, w_in, w_gate_up, b_gate, conv_w, gla_norm_g, w_mem_kv, w_out,
           ln1_g, ln1_b, w_ff1, w_ff2, ln2_g, ln2_b):
    depth = w_in.shape[0]
    bsz, s, d = x.shape
    alpha = (2.0 * depth) ** 0.25
    for l in range(depth):
        w_in_r, w_kv, w_o = _prep_weights(w_in[l].T, w_gate_up[l], w_mem_kv[l], w_out[l])
        x, w1, w2 = _mixer(alpha, x, w_in_r, b_gate[l][None, :], conv_w[l],
                           gla_norm_g[l][None, :], mem, w_kv, w_o,
                           ln1_g[l][None, :], ln1_b[l][None, :], w_ff1[l], w_ff2[l])
        x = _ffn(alpha, x.reshape(bsz * s, d), w1, w2,
                 ln2_g[l][None, :], ln2_b[l][None, :]).reshape(bsz, s, d)
    return x
```

```python
import functools

import jax
import jax.numpy as jnp
from jax import lax
from jax.experimental import pallas as pl
from jax.experimental.pallas import tpu as pltpu

F32 = jnp.float32
BF16 = jnp.bfloat16

GLA_HEADS = 4
GLA_DK = 64
GLA_DV = 128
GLA_KW = GLA_HEADS * GLA_DK
GLA_WIDTH = GLA_HEADS * GLA_DV
GLA_GATE_RANK = 16
GLA_TAU = 16.0
CONV_CH = 256
CONV_TAPS = 3
XA_HEADS = 4
XA_DH = 64
XA_WIDTH = XA_HEADS * XA_DH
LN_EPS = 1e-5
RMS_EPS = 1e-6
IN_SIZES = (GLA_KW, GLA_KW, GLA_WIDTH, GLA_WIDTH, GLA_GATE_RANK,
            CONV_CH, CONV_CH, CONV_CH, XA_WIDTH)

LANES = 128
SUBLANES = 8
VMEM_LIMIT_BYTES = 60 * 1024 * 1024

OFF_Q = 0
OFF_K = OFF_Q + GLA_KW
OFF_V = OFF_K + GLA_KW
OFF_G = OFF_V + GLA_WIDTH
OFF_CB = OFF_G + GLA_WIDTH
OFF_CC = OFF_CB + CONV_CH
OFF_CH = OFF_CC + CONV_CH
OFF_XQ = OFF_CH + CONV_CH
OFF_GATE = OFF_XQ + XA_WIDTH
IN_COLS_R = OFF_GATE + GLA_KW
GLA_COLS = OFF_CB

MIX_TM = 1024
GLA_C = 128
OUT_SUB = 4
FFN_TM = 1024
FFN_SUB = 8
SAFE_DECAY = 60.0

_NT = (((1,), (1,)), ((), ()))


def _layer_norm(r, g, b):
    mu = jnp.mean(r, axis=-1, keepdims=True)
    d = r - mu
    var = jnp.mean(d * d, axis=-1, keepdims=True)
    return d * lax.rsqrt(var + LN_EPS) * g + b


def _w_prep_kernel(wt_ref, wg_ref, wkv_ref, wout_ref, o_ref, wkvb_ref, woutb_ref):
    wkvb_ref[...] = wkv_ref[...].astype(BF16)
    woutb_ref[...] = wout_ref[...].astype(BF16)
    a0, rank = OFF_CB, GLA_GATE_RANK
    for j in range(0, OFF_GATE, LANES):
        src = j if j < a0 else j + rank
        o_ref[:, j:j + LANES] = wt_ref[src:src + LANES, :].T.astype(BF16)
    pad_rows = LANES - rank
    wa_t = jnp.concatenate([wt_ref[a0:a0 + rank, :], jnp.zeros((pad_rows, wt_ref.shape[1]), F32)],
                           axis=0)
    wg = jnp.concatenate([wg_ref[...], jnp.zeros((pad_rows, GLA_KW), F32)], axis=0)
    gate_t = jnp.dot(wg.T, wa_t, preferred_element_type=F32,
                     precision=lax.Precision.HIGHEST)
    o_ref[:, OFF_GATE:OFF_GATE + GLA_KW] = gate_t.T.astype(BF16)


def _head_masks():
    lane = lax.broadcasted_iota(jnp.int32, (1, GLA_KW), 1)
    return [(lane // GLA_DK) == h for h in range(GLA_HEADS)]


def _stack_heads(q, head_mask):
    return jnp.concatenate([jnp.where(m, q, 0.0) for m in head_mask], axis=0).astype(BF16)


def _gla_head_out(scores_h, o_inter_h, v_h, g_h, gn, causal):
    a_h = jnp.where(causal, scores_h, 0.0).astype(BF16)
    o_h = jnp.dot(a_h, v_h, preferred_element_type=F32) + o_inter_h
    ms = jnp.mean(o_h * o_h, axis=-1, keepdims=True)
    o_n = o_h * lax.rsqrt(ms + RMS_EPS) * gn
    return (o_n * (g_h * jax.nn.sigmoid(g_h))).astype(BF16)


def _mixer_kernel(alpha, x_ref, win_ref, bg_ref, cw_ref, gn_ref, mem_ref, wkv_ref, wout_ref,
                  g1_ref, b1_ref, wf1_ref, wf2_ref, o_ref, wf1b_ref, wf2b_ref, z_ref, b_ref,
                  sst_ref, y_ref, u_ref, s_ref, sc_ref, mkv_ref):
    wf1b_ref[...] = wf1_ref[...].astype(BF16)
    wf2b_ref[...] = wf2_ref[...].astype(BF16)

    tm = x_ref.shape[1]
    c = GLA_C
    n_chunks = tm // c
    chunk_rows = [slice(ci * c, (ci + 1) * c) for ci in range(n_chunks)]
    head_rows = [slice(h * c, (h + 1) * c) for h in range(GLA_HEADS)]
    head_keys = [slice(h * GLA_DK, (h + 1) * GLA_DK) for h in range(GLA_HEADS)]
    head_vals = [slice(h * GLA_DV, (h + 1) * GLA_DV) for h in range(GLA_HEADS)]

    @pl.when(pl.program_id(1) == 0)
    def _start_of_sequence():
        s_ref[...] = jnp.zeros_like(s_ref)
        u_ref[0:SUBLANES, :] = jnp.zeros((SUBLANES, CONV_CH), F32)
        mkv_ref[...] = jnp.dot(mem_ref[0].astype(BF16), wkv_ref[...],
                               preferred_element_type=F32).astype(BF16)

    xb = x_ref[0].astype(BF16)

    def proj(off, width):
        return jnp.dot(xb, win_ref[:, off:off + width], preferred_element_type=F32)

    def v_of(rows, h):
        return z_ref[rows, OFF_V + h * GLA_DV:OFF_V + (h + 1) * GLA_DV].astype(BF16)

    head_mask = _head_masks()
    row_i = lax.broadcasted_iota(jnp.int32, (c, c), 0)
    col_i = lax.broadcasted_iota(jnp.int32, (c, c), 1)
    causal = row_i >= col_i
    tri = causal.astype(BF16)
    q_scale = GLA_DK ** -0.5
    gn = gn_ref[...]

    pre = proj(OFF_GATE, GLA_KW) + bg_ref[...]
    la = (jnp.minimum(pre, 0.0) - jnp.log(1.0 + jnp.exp(-jnp.abs(pre)))) * (1.0 / GLA_TAU)
    z_ref[:, OFF_Q:OFF_Q + 2 * GLA_KW] = proj(OFF_Q, 2 * GLA_KW)
    for rows in chunk_rows:
        la_c = la[rows, :]
        hi = la_c.astype(BF16)
        lo = (la_c - hi.astype(F32)).astype(BF16)
        b_ref[rows, :] = (jnp.dot(tri, hi, preferred_element_type=F32)
                          + jnp.dot(tri, lo, preferred_element_type=F32))

    z_ref[:, OFF_V:OFF_V + 2 * GLA_WIDTH] = proj(OFF_V, 2 * GLA_WIDTH)
    xq = proj(OFF_XQ, XA_WIDTH) * (XA_DH ** -0.5)

    worst = jnp.zeros((1, GLA_KW), F32)
    qs_chunks, scores_chunks, dec_chunks, upd_chunks = [], [], [], []
    for rows in chunk_rows:
        b = b_ref[rows, :]
        b_last = b[c - 1:c, :]
        worst = jnp.maximum(worst, -b_last)
        kmat = z_ref[rows, OFF_K:OFF_K + GLA_KW]
        qs = _stack_heads(z_ref[rows, OFF_Q:OFF_Q + GLA_KW] * (jnp.exp(b) * q_scale),
                          head_mask)
        kt = (kmat * jnp.exp(-b)).astype(BF16)
        kd_t = (kmat * jnp.exp(b_last - b)).T.astype(BF16)
        qs_chunks.append(qs)
        scores_chunks.append(lax.dot_general(qs, kt, _NT, preferred_element_type=F32))
        dec_chunks.append(jnp.broadcast_to(jnp.exp(b_last), (GLA_DV, GLA_KW)).T)
        upd_chunks.append(jnp.concatenate(
            [jnp.dot(kd_t[head_keys[h], :], v_of(rows, h), preferred_element_type=F32)
             for h in range(GLA_HEADS)], axis=0))

    mk = mkv_ref[:, 0:XA_WIDTH]
    mv = mkv_ref[:, XA_WIDTH:2 * XA_WIDTH].astype(F32)
    probs = []
    for h in range(XA_HEADS):
        q_h = jnp.where(head_mask[h], xq, 0.0).astype(BF16)
        s = lax.dot_general(q_h, mk, _NT, preferred_element_type=F32)
        e = jnp.exp(s - jnp.max(s, axis=-1, keepdims=True))
        probs.append((e * (1.0 / jnp.sum(e, axis=-1, keepdims=True))).astype(BF16))


    s_c = s_ref[...]
    for ci, rows in enumerate(chunk_rows):
        sst_ref[ci] = s_c
        o_inter = jnp.dot(qs_chunks[ci], s_c.astype(BF16), preferred_element_type=F32)
        for h in range(GLA_HEADS):
            y_ref[rows, head_vals[h]] = _gla_head_out(
                scores_chunks[ci][head_rows[h], :], o_inter[head_rows[h], :],
                v_of(rows, h), z_ref[rows, OFF_G + h * GLA_DV:OFF_G + (h + 1) * GLA_DV],
                gn, causal)
        s_c = s_c * dec_chunks[ci] + upd_chunks[ci]
    s_ref[...] = s_c

    zc = proj(OFF_CB, 3 * CONV_CH)
    u = zc[:, CONV_CH:2 * CONV_CH] * zc[:, 2 * CONV_CH:]
    u_ref[SUBLANES:SUBLANES + tm, :] = u
    u_m1 = u_ref[SUBLANES - 1:SUBLANES - 1 + tm, :]
    u_m2 = u_ref[SUBLANES - 2:SUBLANES - 2 + tm, :]
    conv = cw_ref[0, 0:1, :] * u_m2 + cw_ref[0, 1:2, :] * u_m1 + cw_ref[0, 2:3, :] * u
    y_conv = (zc[:, 0:CONV_CH] * conv).astype(BF16)
    y_ref[:, GLA_WIDTH:GLA_WIDTH + CONV_CH] = y_conv
    u_ref[0:SUBLANES, :] = u_ref[tm:tm + SUBLANES, :]

    xa = jnp.zeros((tm, XA_WIDTH), F32)
    for h in range(XA_HEADS):
        mv_h = jnp.where(head_mask[h], mv, 0.0).astype(BF16)
        xa = xa + jnp.dot(probs[h], mv_h, preferred_element_type=F32)
    y_xa = xa.astype(BF16)
    y_ref[:, GLA_WIDTH + CONV_CH:] = y_xa

    sub = tm // OUT_SUB
    for i in range(OUT_SUB):
        rows = slice(i * sub, (i + 1) * sub)
        mix = jnp.dot(y_ref[rows, :], wout_ref[...], preferred_element_type=F32)
        o_ref[0, rows, :] = _layer_norm(alpha * x_ref[0, rows, :] + mix,
                                        g1_ref[...], b1_ref[...])

    @pl.when(jnp.max(worst) > SAFE_DECAY)
    def _redo_with_rowwise_scores():
        sub8 = lax.broadcasted_iota(jnp.int32, (SUBLANES, GLA_KW), 0)
        lane8 = lax.broadcasted_iota(jnp.int32, (SUBLANES, GLA_KW), 1)
        head_sel8 = ((lane8 // GLA_DK) == sub8).astype(F32)

        def chunk_body(ci, carry):
            r0 = pl.multiple_of(ci * c, c)
            rows = pl.ds(r0, c)
            b = b_ref[rows, :]
            kmat = z_ref[rows, OFF_K:OFF_K + GLA_KW]

            def row_body(i, carry2):
                b_i = b_ref[pl.ds(r0 + i, 1), :]
                q_i = z_ref[pl.ds(r0 + i, 1), OFF_Q:OFF_Q + GLA_KW] * q_scale
                w = (kmat * jnp.exp(jnp.minimum(b_i - b, 0.0))).astype(BF16)
                q8 = (jnp.broadcast_to(q_i, (SUBLANES, GLA_KW)) * head_sel8).astype(BF16)
                r = lax.dot_general(q8, w, _NT, preferred_element_type=F32)
                for h in range(GLA_HEADS):
                    sc_ref[pl.ds(h * c + i, 1), :] = r[h:h + 1, :]
                return carry2

            lax.fori_loop(0, c, row_body, 0)
            qs = _stack_heads(z_ref[rows, OFF_Q:OFF_Q + GLA_KW] * (jnp.exp(b) * q_scale),
                              head_mask)
            o_inter = jnp.dot(qs, sst_ref[ci].astype(BF16), preferred_element_type=F32)
            for h in range(GLA_HEADS):
                y_ref[rows, head_vals[h]] = _gla_head_out(
                    sc_ref[head_rows[h], :], o_inter[head_rows[h], :], v_of(rows, h),
                    z_ref[rows, OFF_G + h * GLA_DV:OFF_G + (h + 1) * GLA_DV], gn, causal)
            return carry

        lax.fori_loop(0, n_chunks, chunk_body, 0)
        mix = jnp.dot(y_ref[...], wout_ref[...], preferred_element_type=F32)
        o_ref[0] = _layer_norm(alpha * x_ref[0] + mix, g1_ref[...], b1_ref[...])


def _ffn_kernel(alpha, x_ref, w1_ref, w2_ref, g2_ref, b2_ref, o_ref):
    sub = x_ref.shape[0] // FFN_SUB
    for i in range(FFN_SUB):
        rows = slice(i * sub, (i + 1) * sub)
        x = x_ref[rows, :]
        h = jnp.dot(x.astype(BF16), w1_ref[...], preferred_element_type=F32)
        h = jnp.square(jnp.maximum(h, 0.0)).astype(BF16)
        ff = jnp.dot(h, w2_ref[...], preferred_element_type=F32)
        o_ref[rows, :] = _layer_norm(alpha * x + ff, g2_ref[...], b2_ref[...])


def _const_spec(shape):
    return pl.BlockSpec(shape, lambda *_: (0,) * len(shape))


def _prep_weights(w_in_t, w_g, w_kv, w_out):
    assert w_in_t.shape[0] == sum(IN_SIZES) and sum(IN_SIZES[:4]) == OFF_CB
    return pl.pallas_call(
        _w_prep_kernel,
        out_shape=[jax.ShapeDtypeStruct((w_in_t.shape[1], IN_COLS_R), BF16),
                   jax.ShapeDtypeStruct(w_kv.shape, BF16),
                   jax.ShapeDtypeStruct(w_out.shape, BF16)],
        compiler_params=pltpu.CompilerParams(vmem_limit_bytes=VMEM_LIMIT_BYTES),
        name="w_prep",
    )(w_in_t, w_g, w_kv, w_out)


def _mixer(alpha, layer, x, w_in_r, bg, cw, gn, mem, w_kv, w_out, g1, b1, w_ff1, w_ff2):
    bsz, s, d = x.shape
    m = mem.shape[1]
    tm = MIX_TM
    tps = s // tm
    n_steps = bsz * tps
    assert s % tm == 0 and tm % GLA_C == 0 and (tm // OUT_SUB) % GLA_C == 0
    f1_rows, f2_rows = w_ff1.shape[0] // n_steps, w_ff2.shape[0] // n_steps
    assert f1_rows * n_steps == w_ff1.shape[0] and f1_rows % (2 * SUBLANES) == 0
    assert f2_rows * n_steps == w_ff2.shape[0] and f2_rows % (2 * SUBLANES) == 0

    def step_rows(b, j):
        return (b * tps + j, 0)

    return pl.pallas_call(
        functools.partial(_mixer_kernel, alpha),
        grid=(bsz, tps),
        in_specs=[
            pl.BlockSpec((1, tm, d), lambda b, j: (b, j, 0)),
            _const_spec(w_in_r.shape), _const_spec(bg.shape),
            pl.BlockSpec((1,) + cw.shape[1:], lambda b, j: (layer, 0, 0)), _const_spec(gn.shape),
            pl.BlockSpec((1, m, d), lambda b, j: (b, 0, 0)),
            _const_spec(w_kv.shape), _const_spec(w_out.shape),
            _const_spec(g1.shape), _const_spec(b1.shape),
            pl.BlockSpec((f1_rows, w_ff1.shape[1]), step_rows),
            pl.BlockSpec((f2_rows, w_ff2.shape[1]), step_rows),
        ],
        out_specs=[
            pl.BlockSpec((1, tm, d), lambda b, j: (b, j, 0)),
            pl.BlockSpec((f1_rows, w_ff1.shape[1]), step_rows),
            pl.BlockSpec((f2_rows, w_ff2.shape[1]), step_rows),
        ],
        out_shape=[
            jax.ShapeDtypeStruct((bsz, s, d), F32),
            jax.ShapeDtypeStruct(w_ff1.shape, BF16),
            jax.ShapeDtypeStruct(w_ff2.shape, BF16),
        ],
        scratch_shapes=[
            pltpu.VMEM((tm, GLA_COLS), F32),
            pltpu.VMEM((tm, GLA_KW), F32),
            pltpu.VMEM((tm // GLA_C, GLA_KW, GLA_DV), F32),
            pltpu.VMEM((tm, d), BF16),
            pltpu.VMEM((tm + SUBLANES, CONV_CH), F32),
            pltpu.VMEM((GLA_KW, GLA_DV), F32),
            pltpu.VMEM((GLA_HEADS * GLA_C, GLA_C), F32),
            pltpu.VMEM((m, 2 * XA_WIDTH), BF16),
        ],
        compiler_params=pltpu.CompilerParams(
            dimension_semantics=("arbitrary", "arbitrary"),
            vmem_limit_bytes=VMEM_LIMIT_BYTES),
        name="mixer",
    )(x, w_in_r, bg, cw, gn, mem, w_kv, w_out, g1, b1, w_ff1, w_ff2)


def _ffn(alpha, x2d, w1, w2, g2, b2):
    t, d = x2d.shape
    tm = FFN_TM
    assert t % tm == 0 and tm % FFN_SUB == 0
    return pl.pallas_call(
        functools.partial(_ffn_kernel, alpha),
        grid=(t // tm,),
        in_specs=[
            pl.BlockSpec((tm, d), lambda i: (i, 0)),
            _const_spec(w1.shape), _const_spec(w2.shape),
            _const_spec(g2.shape), _const_spec(b2.shape),
        ],
        out_specs=pl.BlockSpec((tm, d), lambda i: (i, 0)),
        out_shape=jax.ShapeDtypeStruct((t, d), F32),
        compiler_params=pltpu.CompilerParams(
            dimension_semantics=("arbitrary",), vmem_limit_bytes=VMEM_LIMIT_BYTES),
        name="ffn",
    )(x2d, w1, w2, g2, b2)


def kernel(x, mem, w_in, w_gate_up, b_gate, conv_w, gla_norm_g, w_mem_kv, w_out,
           ln1_g, ln1_b, w_ff1, w_ff2, ln2_g, ln2_b):
    depth = w_in.shape[0]
    bsz, s, d = x.shape
    alpha = (2.0 * depth) ** 0.25
    for l in range(depth):
        w_in_r, w_kv, w_o = _prep_weights(w_in[l].T, w_gate_up[l], w_mem_kv[l], w_out[l])
        x, w1, w2 = _mixer(alpha, l, x, w_in_r, b_gate[l][None, :], conv_w,
                           gla_norm_g[l][None, :], mem, w_kv, w_o,
                           ln1_g[l][None, :], ln1_b[l][None, :], w_ff1[l], w_ff2[l])
        x = _ffn(alpha, x.reshape(bsz * s, d), w1, w2,
                 ln2_g[l][None, :], ln2_b[l][None, :]).reshape(bsz, s, d)
    return x
```
